```python
import jax, jax.numpy as jnp
from jax import lax
import numpy as np

D_MODEL = 1024
BATCH = 16
SEQ = 2048
DEPTH = 4

GRID_W = 64
CTX_LEN = 256
EPS = 1e-6
N_MOD = 6

N_BRANCH = 4
BRANCH_W = 512

N_HEADS = 8
Q_LORA = 256
KV_LORA = 128
NOPE_DIM = 64
ROPE_DIM = 32
V_DIM = BRANCH_W // N_HEADS
QK_DIM = NOPE_DIM + ROPE_DIM
ROPE_HALF = ROPE_DIM // 2
AXIS_PAIRS = ROPE_DIM // 4
ROPE_THETA = 10000.0
Q_BLOCK = 128
ATTN_SCALE = QK_DIM ** -0.5

CONV_W = BRANCH_W
CONV_K = 3

POOL_WINDOWS = (2, 4, 8, 16)
N_POOL_G = len(POOL_WINDOWS)
POOL_GC = BRANCH_W // N_POOL_G
POOL_W = BRANCH_W

SGU_CHUNK = 128
SGU_G = 4
SGU_GC = BRANCH_W // SGU_G
SGU_W = BRANCH_W

KV_OFF = 0
KR_OFF = KV_OFF + KV_LORA
Q_OFF = KR_OFF + ROPE_DIM
CONV_OFF = Q_OFF + Q_LORA
POOL_OFF = CONV_OFF + 3 * CONV_W
SGU_OFF = POOL_OFF + POOL_W
GATE_OFF = SGU_OFF + 2 * SGU_W
N_IN = GATE_OFF + N_BRANCH * D_MODEL

N_EXPERTS = 16
N_EXPERT_GROUPS = 4
EXPERTS_PER_GROUP = N_EXPERTS // N_EXPERT_GROUPS
GROUP_SCORE_TOPK = 2
TOP_K = 2
EXPERT_FF = 256

kernel_name = "hybrid_parallel_mixer_moe_diffusion_trunk"


def rms_norm(x, g):
    xf = x.astype(jnp.float32)
    y = xf * lax.rsqrt(jnp.mean(xf * xf, axis=-1, keepdims=True) + EPS)
    return (y * g.astype(jnp.float32)).astype(x.dtype)


def modulate(xn, shift, scale):
    return xn * (1.0 + scale) + shift


def axial_rope(length):
    rows = length // GRID_W
    grid = jnp.stack(jnp.meshgrid(jnp.arange(rows), jnp.arange(GRID_W), indexing='ij'), -1)
    grid = grid.reshape(rows * GRID_W, 2).astype(jnp.float32)
    inv_freq = ROPE_THETA ** (-jnp.arange(AXIS_PAIRS, dtype=jnp.float32) / AXIS_PAIRS)
    ang = jnp.concatenate([grid[:, :1] * inv_freq, grid[:, 1:] * inv_freq], -1)
    return jnp.cos(ang), jnp.sin(ang)


def apply_rope(t, cos, sin):
    nope, r = t[..., :NOPE_DIM], t[..., NOPE_DIM:]
    r1, r2 = r[..., :ROPE_HALF], r[..., ROPE_HALF:]
    cs = cos[None, :, None, :].astype(t.dtype)
    sn = sin[None, :, None, :].astype(t.dtype)
    return jnp.concatenate([nope, r1 * cs - r2 * sn, r1 * sn + r2 * cs], -1)


def mla_keys_values(p, kv_a_norm_g, w_ukv, k_norm_g, rope):
    ckv = rms_norm(p[..., KV_OFF:KR_OFF], kv_a_norm_g)
    kv = jnp.einsum('blr,rhe->blhe', ckv, w_ukv)
    k_nope, v = kv[..., :NOPE_DIM], kv[..., NOPE_DIM:]
    k_rope = jnp.broadcast_to(p[..., None, KR_OFF:Q_OFF], k_nope.shape[:-1] + (ROPE_DIM,))
    k = rms_norm(jnp.concatenate([k_nope, k_rope], -1), k_norm_g)
    if rope is not None:
        k = apply_rope(k, *rope)
    return k.transpose(0, 2, 1, 3), v.transpose(0, 2, 1, 3)


def mla_queries(p, q_a_norm_g, w_uq, q_norm_g, rope):
    cq = rms_norm(p[..., Q_OFF:CONV_OFF], q_a_norm_g)
    q = rms_norm(jnp.einsum('blr,rhe->blhe', cq, w_uq), q_norm_g)
    if rope is not None:
        q = apply_rope(q, *rope)
    return q.transpose(0, 2, 1, 3)


def block_attention(q, k, v):
    nb_b, n_h, length, _ = q.shape
    nb = length // Q_BLOCK
    qb = jnp.moveaxis(q.reshape(nb_b, n_h, nb, Q_BLOCK, QK_DIM), 2, 0)

    def one_block(qi):
        s = jnp.einsum('bhqd,bhkd->bhqk', qi, k, preferred_element_type=jnp.float32) * ATTN_SCALE
        pr = jax.nn.softmax(s, axis=-1).astype(v.dtype)
        return jnp.einsum('bhqk,bhkd->bhqd', pr, v)

    o = lax.map(one_block, qb)
    return jnp.moveaxis(o, 0, 2).reshape(nb_b, n_h, length, V_DIM)


def short_conv(u, w):
    return lax.conv_general_dilated(
        u, w[:, None, :].astype(u.dtype), window_strides=(1,), padding=((CONV_K // 2, CONV_K // 2),),
        dimension_numbers=('NWC', 'WIO', 'NWC'), feature_group_count=u.shape[-1])


def pool_branch(p, pool_w, pool_scale):
    nb_b, length, _ = p.shape
    pf = p.astype(jnp.float32)
    cs = jnp.pad(jnp.cumsum(pf, axis=1), ((0, 0), (1, 0), (0, 0)))
    t = jnp.arange(length)
    means = []
    for g, w in enumerate(POOL_WINDOWS):
        lo = jnp.clip(t - w // 2, 0, length)
        hi = jnp.clip(t - w // 2 + w, 0, length)
        csg = cs[..., g * POOL_GC:(g + 1) * POOL_GC]
        cnt = (hi - lo).astype(jnp.float32)[None, :, None]
        means.append((csg[:, hi] - csg[:, lo]) / cnt)
    pooled = (jnp.concatenate(means, -1) - pf).astype(p.dtype)
    mixed = jnp.einsum('blgc,gcd->blgd', pooled.reshape(nb_b, length, N_POOL_G, POOL_GC), pool_w)
    return mixed.reshape(nb_b, length, POOL_W) * pool_scale


def sgu_branch(z, sgu_norm_g, sgu_w, sgu_b):
    z = jax.nn.gelu(z)
    u, v = z[..., :SGU_W], z[..., SGU_W:]
    v = rms_norm(v, sgu_norm_g)
    nb_b, length, _ = v.shape
    vc = v.reshape(nb_b, length // SGU_CHUNK, SGU_CHUNK, SGU_G, SGU_GC)
    mixed = jnp.einsum('gpq,bnqgc->bnpgc', sgu_w, vc) + sgu_b.T[None, None, :, :, None]
    return u * mixed.reshape(nb_b, length, SGU_W)


def branch_merge(p, k_all, v_all, rope, q_a_norm_g, w_uq, q_norm_g, conv_w, pool_w, pool_scale,
                 sgu_norm_g, sgu_w, sgu_b, w_branch, w_o):
    nb_b, length, _ = p.shape
    q = mla_queries(p, q_a_norm_g, w_uq, q_norm_g, rope)
    o = block_attention(q, k_all, v_all)
    br_attn = o.transpose(0, 2, 1, 3).reshape(nb_b, length, BRANCH_W)
    gb, gc, u = jnp.split(p[..., CONV_OFF:POOL_OFF], 3, axis=-1)
    br_conv = gb * short_conv(gc * u, conv_w)
    br_pool = pool_branch(p[..., POOL_OFF:SGU_OFF], pool_w, pool_scale)
    br_sgu = sgu_branch(p[..., SGU_OFF:GATE_OFF], sgu_norm_g, sgu_w, sgu_b)
    gates = jax.nn.sigmoid(p[..., GATE_OFF:].reshape(nb_b, length, N_BRANCH, D_MODEL))
    br = jnp.stack([br_attn, br_conv, br_pool, br_sgu], axis=2)
    merged = jnp.sum(gates * jnp.einsum('blnc,ncd->blnd', br, w_branch), axis=2)
    return merged @ w_o


def grouped_moe(h, router_w, router_b, w1, w3, w2):
    logits = jnp.einsum('bld,de->ble', h, router_w, preferred_element_type=jnp.float32)
    s = jax.nn.sigmoid(logits)
    sb = s + router_b.astype(jnp.float32)
    grp = sb.reshape(sb.shape[:-1] + (N_EXPERT_GROUPS, EXPERTS_PER_GROUP))
    gscore = jnp.sum(lax.top_k(grp, GROUP_SCORE_TOPK)[0], axis=-1)
    gsel = jnp.argmax(gscore, axis=-1)
    in_group = (jnp.arange(N_EXPERTS) // EXPERTS_PER_GROUP) == gsel[..., None]
    _, idx = lax.top_k(jnp.where(in_group, sb, -jnp.inf), TOP_K)
    w = jnp.take_along_axis(s, idx, axis=-1)
    w = w / jnp.sum(w, axis=-1, keepdims=True)
    combine = jnp.sum(jax.nn.one_hot(idx, N_EXPERTS, dtype=jnp.float32) * w[..., None], axis=-2)
    combine = combine.astype(h.dtype)
    y = jnp.zeros_like(h)
    for e in range(N_EXPERTS):
        a = jax.nn.silu(h @ w1[e]) * (h @ w3[e])
        y = y + combine[..., e:e + 1] * (a @ w2[e])
    return y


def setup_inputs(seed: int = 0) -> dict:
    key = jax.random.key(seed)
    ks = iter(jax.random.split(key, 32))
    f32 = jnp.float32

    def nrm(shape, scale):
        return jax.random.normal(next(ks), shape, f32) * scale

    def gain(shape):
        return 1.0 + nrm(shape, 0.05)

    return {
        "x": nrm((BATCH, SEQ, D_MODEL), 1.0),
        "c": nrm((BATCH, D_MODEL), 1.0),
        "ctx": nrm((BATCH, CTX_LEN, D_MODEL), 1.0),
        "c_ctx": nrm((D_MODEL,), 1.0),
        "mod_w": nrm((DEPTH, D_MODEL, N_MOD * D_MODEL), 0.5 * D_MODEL ** -0.5),
        "mod_b": nrm((DEPTH, N_MOD * D_MODEL), 0.01),
        "norm1_g": gain((DEPTH, D_MODEL)),
        "norm2_g": gain((DEPTH, D_MODEL)),
        "w_in": nrm((DEPTH, D_MODEL, N_IN), D_MODEL ** -0.5),
        "kv_a_norm_g": gain((DEPTH, KV_LORA)),
        "w_ukv": nrm((DEPTH, KV_LORA, N_HEADS, NOPE_DIM + V_DIM), KV_LORA ** -0.5),
        "k_norm_g": gain((DEPTH, QK_DIM)),
        "q_a_norm_g": gain((DEPTH, Q_LORA)),
        "w_uq": nrm((DEPTH, Q_LORA, N_HEADS, QK_DIM), Q_LORA ** -0.5),
        "q_norm_g": gain((DEPTH, QK_DIM)),
        "conv_w": nrm((DEPTH, CONV_K, CONV_W), CONV_K ** -0.5),
        "pool_w": nrm((DEPTH, N_POOL_G, POOL_GC, POOL_GC), POOL_GC ** -0.5),
        "pool_scale": gain((DEPTH, POOL_W)),
        "sgu_norm_g": gain((DEPTH, SGU_W)),
        "sgu_w": nrm((DEPTH, SGU_G, SGU_CHUNK, SGU_CHUNK), SGU_CHUNK ** -0.5),
        "sgu_b": gain((DEPTH, SGU_G, SGU_CHUNK)),
        "w_branch": nrm((DEPTH, N_BRANCH, BRANCH_W, D_MODEL), BRANCH_W ** -0.5),
        "w_o": nrm((DEPTH, D_MODEL, D_MODEL), D_MODEL ** -0.5),
        "router_w": nrm((D_MODEL, N_EXPERTS), D_MODEL ** -0.5),
        "router_b": nrm((N_EXPERTS,), 0.01),
        "moe_w1": nrm((DEPTH, N_EXPERTS, D_MODEL, EXPERT_FF), D_MODEL ** -0.5),
        "moe_w3": nrm((DEPTH, N_EXPERTS, D_MODEL, EXPERT_FF), D_MODEL ** -0.5),
        "moe_w2": nrm((DEPTH, N_EXPERTS, EXPERT_FF, D_MODEL), EXPERT_FF ** -0.5),
    }


def reference(x, c, ctx, c_ctx, mod_w, mod_b, norm1_g, norm2_g, w_in, kv_a_norm_g, w_ukv, k_norm_g,
              q_a_norm_g, w_uq, q_norm_g, conv_w, pool_w, pool_scale, sgu_norm_g, sgu_w, sgu_b,
              w_branch, w_o, router_w, router_b, moe_w1, moe_w3, moe_w2):
    nb_b, length, d = x.shape
    rope = axial_rope(length)
    sc = jax.nn.silu(c)
    scc = jax.nn.silu(c_ctx)
    xc = ctx
    for i in range(DEPTH):
        update_ctx = i < DEPTH - 1
        m = (sc @ mod_w[i] + mod_b[i]).reshape(nb_b, N_MOD, 1, d)
        mc = (scc @ mod_w[i] + mod_b[i]).reshape(N_MOD, d)
        h = modulate(rms_norm(x, norm1_g[i]), m[:, 0], m[:, 1])
        hc = modulate(rms_norm(xc, norm1_g[i]), mc[0], mc[1])
        p = h @ w_in[i]
        pc = hc @ (w_in[i] if update_ctx else w_in[i][:, :Q_OFF])
        k_c, v_c = mla_keys_values(pc, kv_a_norm_g[i], w_ukv[i], k_norm_g[i], None)
        k_l, v_l = mla_keys_values(p, kv_a_norm_g[i], w_ukv[i], k_norm_g[i], rope)
        k_all = jnp.concatenate([k_c, k_l], axis=2)
        v_all = jnp.concatenate([v_c, v_l], axis=2)
        mix = branch_merge(p, k_all, v_all, rope, q_a_norm_g[i], w_uq[i], q_norm_g[i], conv_w[i],
                           pool_w[i], pool_scale[i], sgu_norm_g[i], sgu_w[i], sgu_b[i],
                           w_branch[i], w_o[i])
        x = x + m[:, 2] * mix
        h2 = modulate(rms_norm(x, norm2_g[i]), m[:, 3], m[:, 4])
        x = x + m[:, 5] * grouped_moe(h2, router_w, router_b, moe_w1[i], moe_w3[i], moe_w2[i])
        if update_ctx:
            mix_c = branch_merge(pc, k_c, v_c, None, q_a_norm_g[i], w_uq[i], q_norm_g[i], conv_w[i],
                                 pool_w[i], pool_scale[i], sgu_norm_g[i], sgu_w[i], sgu_b[i],
                                 w_branch[i], w_o[i])
            xc = xc + mc[2] * mix_c
            hc2 = modulate(rms_norm(xc, norm2_g[i]), mc[3], mc[4])
            xc = xc + mc[5] * grouped_moe(hc2, router_w, router_b, moe_w1[i], moe_w3[i], moe_w2[i])
    return x
```

```python
import functools

import jax
import jax.numpy as jnp
from jax import lax
from jax.experimental import pallas as pl
from jax.experimental.pallas import tpu as pltpu

F32 = jnp.float32
BF16 = jnp.bfloat16

D_MODEL = 1024
DEPTH = 4
GRID_W = 64
EPS = 1e-6
N_MOD = 6
N_BRANCH = 4
BRANCH_W = 512
N_HEADS = 8
Q_LORA = 256
KV_LORA = 128
NOPE_DIM = 64
ROPE_DIM = 32
V_DIM = 64
QK_DIM = NOPE_DIM + ROPE_DIM
ROPE_HALF = ROPE_DIM // 2
AXIS_PAIRS = ROPE_DIM // 4
ROPE_THETA = 10000.0
ATTN_SCALE = QK_DIM ** -0.5
CONV_K = 3
POOL_WINDOWS = (2, 4, 8, 16)
POOL_GC = 128
SGU_CHUNK = 128
SGU_G = 4
N_EXPERTS = 16
N_EXPERT_GROUPS = 4
EXPERTS_PER_GROUP = 4
EXPERT_FF = 256

KV_OFF = 0
KR_OFF = KV_OFF + KV_LORA
Q_OFF = KR_OFF + ROPE_DIM
CONV_OFF = Q_OFF + Q_LORA
POOL_OFF = CONV_OFF + 3 * BRANCH_W
SGU_OFF = POOL_OFF + BRANCH_W
GATE_OFF = SGU_OFF + 2 * BRANCH_W

LANES = 128
HALO = 16
TOKEN_TILE = 256
VMEM_LIMIT = 56 * 1024 * 1024


def _rms(x, width=None):
    width = x.shape[-1] if width is None else width
    return x * lax.rsqrt(jnp.sum(x * x, axis=-1, keepdims=True) * (1.0 / width) + EPS)


def _norm_mod(x, g, shift, scale):
    return (_rms(x) * g) * (1.0 + scale) + shift


def _dot(a, b):
    return jnp.dot(a, b, preferred_element_type=F32)


def _const_spec(shape, index_map):
    return pl.BlockSpec(shape, index_map, pipeline_mode=pl.Buffered(1))


def _mod_kernel(c_ref, w_ref, b_ref, o_ref):
    c = c_ref[...]
    sc = c * jax.nn.sigmoid(c)
    o_ref[...] = jnp.dot(sc, w_ref[...], preferred_element_type=F32,
                         precision=lax.Precision.HIGHEST) + b_ref[...]


def _modulation(cvec, mod_w, mod_b):
    rows = cvec.shape[0]
    n = mod_w.shape[-1]
    bn = 1536
    return pl.pallas_call(
        _mod_kernel,
        grid=(DEPTH, n // bn),
        in_specs=[
            pl.BlockSpec((rows, D_MODEL), lambda i, j: (0, 0)),
            pl.BlockSpec((None, D_MODEL, bn), lambda i, j: (i, 0, j)),
            pl.BlockSpec((None, 1, bn), lambda i, j: (i, 0, j)),
        ],
        out_specs=pl.BlockSpec((None, rows, bn), lambda i, j: (i, 0, j)),
        out_shape=jax.ShapeDtypeStruct((DEPTH, rows, n), F32),
        compiler_params=pltpu.CompilerParams(
            dimension_semantics=("arbitrary", "arbitrary"), vmem_limit_bytes=VMEM_LIMIT),
    )(cvec, mod_w, mod_b.reshape(DEPTH, 1, n))


def _qkv_kernel(x_ref, mod_ref, g1_ref, wa_ref, gkv_ref, gq_ref, wuk_ref, wuv_ref, wuq_ref,
                gk_ref, gqn_ref, rc_ref, rsa_ref, rsb_ref, q_ref, k_ref, v_ref):
    x = x_ref[...]
    m = mod_ref[...]
    h = _norm_mod(x, g1_ref[...], m[0:1], m[1:2]).astype(BF16)
    p = _dot(h, wa_ref[...])
    ckv = (_rms(p[:, :KV_LORA]) * gkv_ref[...]).astype(BF16)
    kk = _dot(ckv, wuk_ref[...])
    vv = _dot(ckv, wuv_ref[...])
    kr = p[:, LANES:2 * LANES]
    cq = (_rms(p[:, 2 * LANES:]) * gq_ref[...]).astype(BF16)
    qq = _dot(cq, wuq_ref[...])
    rc, rsa, rsb = rc_ref[...], rsa_ref[...], rsb_ref[...]

    def rope(t):
        return (t * rc + pltpu.roll(t, LANES - ROPE_HALF, 1) * rsa
                + pltpu.roll(t, ROPE_HALF, 1) * rsb)

    for hd in range(N_HEADS):
        kh = kk[:, hd * LANES:(hd + 1) * LANES] + kr
        kh = _rms(kh, QK_DIM) * gk_ref[...]
        k_ref[hd] = rope(kh).astype(BF16)
        qh = _rms(qq[:, hd * LANES:(hd + 1) * LANES], QK_DIM) * gqn_ref[...]
        q_ref[hd] = (rope(qh) * ATTN_SCALE).astype(BF16)
        v_ref[hd] = vv[:, hd * V_DIM:(hd + 1) * V_DIM].astype(BF16)


def _qkv(xs, mod, lw, i, rope_tabs, n_ctx_tiles):
    b, s, d = xs.shape
    tm = TOKEN_TILE
    nt = s // tm

    def cw(shape):
        return _const_spec((None,) + shape, lambda bb, j: (i,) + (0,) * len(shape))

    tok = lambda bb, j: (bb, j, 0)
    head = lambda bb, j: (bb, 0, j, 0)
    rspec = pl.BlockSpec((tm, LANES), lambda bb, j: (j, 0))
    return pl.pallas_call(
        _qkv_kernel,
        grid=(b, nt),
        in_specs=[
            pl.BlockSpec((None, tm, d), tok),
            pl.BlockSpec((None, None, None, N_MOD, d),
                         lambda bb, j: (i, bb, jnp.where(j < n_ctx_tiles, 0, 1), 0, 0)),
            cw((1, d)), cw((d, 4 * LANES)), cw((1, KV_LORA)), cw((1, Q_LORA)),
            cw((KV_LORA, N_HEADS * LANES)), cw((KV_LORA, N_HEADS * V_DIM)),
            cw((Q_LORA, N_HEADS * LANES)), cw((1, LANES)), cw((1, LANES)),
            rspec, rspec, rspec,
        ],
        out_specs=[
            pl.BlockSpec((None, N_HEADS, tm, LANES), head),
            pl.BlockSpec((None, N_HEADS, tm, LANES), head),
            pl.BlockSpec((None, N_HEADS, tm, V_DIM), head),
        ],
        out_shape=[
            jax.ShapeDtypeStruct((b, N_HEADS, s, LANES), BF16),
            jax.ShapeDtypeStruct((b, N_HEADS, s, LANES), BF16),
            jax.ShapeDtypeStruct((b, N_HEADS, s, V_DIM), BF16),
        ],
        compiler_params=pltpu.CompilerParams(
            dimension_semantics=("arbitrary", "arbitrary"), vmem_limit_bytes=VMEM_LIMIT),
    )(xs, mod, lw["g1"], lw["w_a"], lw["g_kv"], lw["g_qa"], lw["w_uk"], lw["w_uv"], lw["w_uq"],
      lw["g_k"], lw["g_q"], *rope_tabs)


def _attn_kernel(q_ref, k_ref, v_ref, o_ref, o_scr, *, first_tile, n_ctx_tiles, ctx_len):
    n_keys = k_ref.shape[1]

    def attend(nk):
        def body(hd, carry):
            q = q_ref[hd]
            k = k_ref[hd, pl.ds(0, nk), :]
            v = v_ref[hd, pl.ds(0, nk), :]
            s = lax.dot_general(q, k, (((1,), (1,)), ((), ())), preferred_element_type=F32)
            e = jnp.exp(s - jnp.max(s, axis=-1, keepdims=True))
            den = jnp.sum(e, axis=-1, keepdims=True)
            o_scr[hd] = _dot(e.astype(BF16), v) / den
            return carry
        lax.fori_loop(0, N_HEADS, body, 0)

    if first_tile >= n_ctx_tiles:
        attend(n_keys)
    else:
        j = pl.program_id(1) + first_tile
        pl.when(j < n_ctx_tiles)(lambda: attend(ctx_len))
        pl.when(j >= n_ctx_tiles)(lambda: attend(n_keys))
    for hd in range(N_HEADS):
        o_ref[:, hd * V_DIM:(hd + 1) * V_DIM] = o_scr[hd].astype(BF16)


def _attention(q, k, v, first_tile, n_ctx_tiles):
    b, _, s, _ = q.shape
    tm = TOKEN_TILE
    nt = s // tm
    kern = functools.partial(_attn_kernel, first_tile=first_tile, n_ctx_tiles=n_ctx_tiles,
                             ctx_len=n_ctx_tiles * tm)
    return pl.pallas_call(
        kern,
        grid=(b, nt - first_tile),
        in_specs=[
            pl.BlockSpec((None, N_HEADS, tm, LANES), lambda bb, j: (bb, 0, j + first_tile, 0)),
            pl.BlockSpec((None, N_HEADS, s, LANES), lambda bb, j: (bb, 0, 0, 0)),
            pl.BlockSpec((None, N_HEADS, s, V_DIM), lambda bb, j: (bb, 0, 0, 0)),
        ],
        out_specs=pl.BlockSpec((None, tm, BRANCH_W), lambda bb, j: (bb, j + first_tile, 0)),
        out_shape=jax.ShapeDtypeStruct((b, s, BRANCH_W), BF16),
        scratch_shapes=[pltpu.VMEM((N_HEADS, tm, V_DIM), F32)],
        compiler_params=pltpu.CompilerParams(
            dimension_semantics=("arbitrary", "arbitrary"), vmem_limit_bytes=VMEM_LIMIT),
    )(q, k, v)


def _route(logits, rbias):
    lane = lax.broadcasted_iota(jnp.int32, logits.shape, 1)
    real = lane < N_EXPERTS
    slot = lane % EXPERTS_PER_GROUP
    s = jax.nn.sigmoid(logits)
    neg = jnp.float32(-jnp.inf)
    sb = jnp.where(real, s + rbias, neg)

    def mate(x, dd):
        fwd = pltpu.roll(x, LANES - dd, 1)
        back = pltpu.roll(x, EXPERTS_PER_GROUP - dd, 1)
        return jnp.where(slot + dd < EXPERTS_PER_GROUP, fwd, back)

    mates = [mate(sb, dd) for dd in (1, 2, 3)]
    vals = [sb] + mates
    gscore = None
    for a in range(4):
        for c in range(a + 1, 4):
            pair = vals[a] + vals[c]
            gscore = pair if gscore is None else jnp.maximum(gscore, pair)
    gscore = jnp.where(real, gscore, neg)
    group_ok = real
    for gg in (1, 2, 3):
        lower = pltpu.roll(gscore, EXPERTS_PER_GROUP * gg, 1)
        higher = pltpu.roll(gscore, LANES - EXPERTS_PER_GROUP * gg, 1)
        group_ok = jnp.logical_and(group_ok, jnp.logical_and(gscore > lower, gscore >= higher))
    rank = jnp.zeros(logits.shape, jnp.int32)
    for dd, mv in zip((1, 2, 3), mates):
        wrapped = slot + dd >= EXPERTS_PER_GROUP
        beats = jnp.logical_or(mv > sb, jnp.logical_and(wrapped, mv == sb))
        rank = rank + beats.astype(jnp.int32)
    chosen = jnp.logical_and(rank < 2, group_ok)
    w = jnp.where(chosen, s, 0.0)
    return w / jnp.sum(w, axis=-1, keepdims=True)


def _mix_kernel(x_ref, xp_ref, xn_ref, mod_ref, g1_ref, g2_ref, wh_ref, wm_ref, attn_ref,
                convw_ref, poolw_ref, pscale_ref, gsgu_ref, sguw_ref, sgub_ref, wbr_ref, wo_ref,
                rw_ref, rb_ref, xo_ref, h2_ref, cmb_ref, hcat_ref,
                *, first_tile, n_ctx_tiles, n_tiles):
    tm = x_ref.shape[0]
    j = pl.program_id(1) + first_tile
    m = mod_ref[...]
    g1 = g1_ref[...]
    x = x_ref[...]

    def hnorm(xv):
        return _norm_mod(xv, g1, m[0:1], m[1:2]).astype(BF16)

    hm = hnorm(x)
    hcat_ref[0:HALO, :] = hnorm(xp_ref[...])
    hcat_ref[HALO:HALO + tm, :] = hm
    hcat_ref[HALO + tm:, :] = hnorm(xn_ref[...])

    in_ctx = j < n_ctx_tiles
    seg_first = jnp.where(in_ctx, 0, n_ctx_tiles)
    seg_last = jnp.where(in_ctx, n_ctx_tiles - 1, n_tiles - 1)
    seg_len = (seg_last - seg_first + 1) * tm
    row = lax.broadcasted_iota(jnp.int32, (tm + 2 * HALO, 1), 0)
    keep = jnp.logical_and(jnp.logical_or(row >= HALO, j > seg_first),
                           jnp.logical_or(row < HALO + tm, j < seg_last))

    ph = _dot(hcat_ref[...], wh_ref[...])
    ph = jnp.where(keep, ph, 0.0)
    nrow = tm + 2 * HALO

    def shifted(a, k):
        return pltpu.roll(a, (-k) % nrow, 0)[HALO:HALO + tm]

    cu = ph[:, :BRANCH_W] * ph[:, BRANCH_W:2 * BRANCH_W]
    cw = convw_ref[...]
    conv = shifted(cu, -1) * cw[0:1] + cu[HALO:HALO + tm] * cw[1:2] + shifted(cu, 1) * cw[2:3]
    gb = _dot(hm, wm_ref[:, 0:BRANCH_W])
    br_conv = gb * conv

    pp = ph[:, 2 * BRANCH_W:]
    pos = (j - seg_first) * tm + lax.broadcasted_iota(jnp.int32, (tm, 1), 0)
    level = pp + pltpu.roll(pp, 1, 0)
    pooled = []
    for gi, win in enumerate(POOL_WINDOWS):
        if gi > 0:
            half = win // 4
            level = level[:, POOL_GC:]
            level = pltpu.roll(level, nrow - half, 0) + pltpu.roll(level, half, 0)
        cnt = (jnp.minimum(pos + win // 2, seg_len) - jnp.maximum(pos - win // 2, 0)).astype(F32)
        own = pp[HALO:HALO + tm, gi * POOL_GC:(gi + 1) * POOL_GC]
        pooled.append(level[HALO:HALO + tm, :POOL_GC] / cnt - own)
    br_pool = jnp.concatenate(
        [_dot(pooled[gi].astype(BF16), poolw_ref[gi]) for gi in range(len(POOL_WINDOWS))], axis=-1)
    br_pool = br_pool * pscale_ref[...]

    z = jax.nn.gelu(_dot(hm, wm_ref[:, BRANCH_W:3 * BRANCH_W]), approximate=True)
    zu = z[:, :BRANCH_W]
    zv = (_rms(z[:, BRANCH_W:]) * gsgu_ref[...]).astype(BF16)
    mixed = jnp.concatenate(
        [_dot(sguw_ref[gi], zv[:, gi * LANES:(gi + 1) * LANES]) for gi in range(SGU_G)], axis=-1)
    br_sgu = zu * (mixed + sgub_ref[...])

    branches = (attn_ref[...], br_conv.astype(BF16), br_pool.astype(BF16), br_sgu.astype(BF16))
    merged = None
    for nb, br in enumerate(branches):
        lo = 3 * BRANCH_W + nb * D_MODEL
        gate = jax.nn.sigmoid(_dot(hm, wm_ref[:, lo:lo + D_MODEL]))
        term = gate * _dot(br, wbr_ref[nb])
        merged = term if merged is None else merged + term
    x_new = x + m[2:3] * _dot(merged.astype(BF16), wo_ref[...])
    xo_ref[...] = x_new

    h2 = _norm_mod(x_new, g2_ref[...], m[3:4], m[4:5])
    h2_ref[...] = h2.astype(BF16)
    logits = jnp.dot(h2, rw_ref[...], preferred_element_type=F32, precision=lax.Precision.HIGHEST)
    cmb_ref[...] = _route(logits, rb_ref[...])


def _mix(xs, mod, attn, lw, i, first_tile, n_ctx_tiles, router_w, router_b):
    b, s, d = xs.shape
    tm = TOKEN_TILE
    nt = s // tm
    hb = tm // HALO
    n_hblk = s // HALO

    def cw(shape):
        return _const_spec((None,) + shape, lambda bb, j: (i,) + (0,) * len(shape))

    tok = lambda bb, j: (bb, j + first_tile, 0)
    out_tok = lambda bb, j: (bb, j, 0)
    s_out = s - first_tile * tm
    kern = functools.partial(_mix_kernel, first_tile=first_tile, n_ctx_tiles=n_ctx_tiles, n_tiles=nt)
    return pl.pallas_call(
        kern,
        grid=(b, nt - first_tile),
        in_specs=[
            pl.BlockSpec((None, tm, d), tok),
            pl.BlockSpec((None, HALO, d),
                         lambda bb, j: (bb, jnp.maximum((j + first_tile) * hb - 1, 0), 0)),
            pl.BlockSpec((None, HALO, d),
                         lambda bb, j: (bb, jnp.minimum((j + first_tile + 1) * hb, n_hblk - 1), 0)),
            pl.BlockSpec((None, None, None, N_MOD, d),
                         lambda bb, j: (i, bb, jnp.where(j + first_tile < n_ctx_tiles, 0, 1), 0, 0)),
            cw((1, d)), cw((1, d)),
            cw((d, 3 * BRANCH_W)), cw((d, 3 * BRANCH_W + N_BRANCH * D_MODEL)),
            pl.BlockSpec((None, tm, BRANCH_W), tok),
            cw((CONV_K, BRANCH_W)), cw((len(POOL_WINDOWS), POOL_GC, POOL_GC)), cw((1, BRANCH_W)),
            cw((1, BRANCH_W)), cw((SGU_G, tm, tm)), cw((tm, BRANCH_W)),
            cw((N_BRANCH, BRANCH_W, D_MODEL)), cw((D_MODEL, D_MODEL)),
            _const_spec((d, LANES), lambda bb, j: (0, 0)),
            _const_spec((1, LANES), lambda bb, j: (0, 0)),
        ],
        out_specs=[
            pl.BlockSpec((None, tm, d), out_tok),
            pl.BlockSpec((None, tm, d), out_tok),
            pl.BlockSpec((None, tm, LANES), out_tok),
        ],
        out_shape=[
            jax.ShapeDtypeStruct((b, s_out, d), F32),
            jax.ShapeDtypeStruct((b, s_out, d), BF16),
            jax.ShapeDtypeStruct((b, s_out, LANES), F32),
        ],
        scratch_shapes=[pltpu.VMEM((tm + 2 * HALO, d), BF16)],
        compiler_params=pltpu.CompilerParams(
            dimension_semantics=("arbitrary", "arbitrary"), vmem_limit_bytes=VMEM_LIMIT),
    )(xs, xs, xs, mod, lw["g1"], lw["g2"], lw["w_h"], lw["w_m"], attn,
      lw["conv_w"], lw["pool_w"], lw["pool_scale"], lw["g_sgu"], lw["sgu_w"], lw["sgu_b"],
      lw["w_branch"], lw["w_o"], router_w, router_b)


def _moe_kernel(x_ref, h_ref, cmb_ref, mod_ref, w1_ref, w3_ref, w2_ref, o_ref, *, row_chunk, ctx_rows):
    tile = x_ref.shape[0]
    ec = pl.program_id(2)
    n_exp = w1_ref.shape[-1] // EXPERT_FF

    @pl.when(ec == 0)
    def _():
        o_ref[...] = x_ref[...]

    for r in range(tile // row_chunk):
        rows = pl.ds(r * row_chunk, row_chunk)
        gate = mod_ref[1, N_MOD - 1:N_MOD]
        if ctx_rows:
            pos = (pl.program_id(1) * tile + r * row_chunk
                   + lax.broadcasted_iota(jnp.int32, (row_chunk, 1), 0))
            gate = jnp.where(pos < ctx_rows, mod_ref[0, N_MOD - 1:N_MOD], gate)
        h = h_ref[rows, :]
        a = _dot(h, w1_ref[...])
        a = (a * jax.nn.sigmoid(a)) * _dot(h, w3_ref[...])
        cmb = cmb_ref[rows, :]
        a = jnp.concatenate(
            [a[:, e * EXPERT_FF:(e + 1) * EXPERT_FF] * cmb[:, e:e + 1] for e in range(n_exp)], axis=-1)
        o_ref[rows, :] += gate * _dot(a.astype(BF16), w2_ref[...])


def _moe_tile(s):
    for tile, chunk in ((1152, 384), (1024, 512), (768, 384), (512, 256), (256, 256)):
        if s % tile == 0:
            return tile, chunk
    raise ValueError(f"unsupported sequence length {s}")


def _moe(x_new, h2, cmb, mod, lw, i, ctx_rows, chunk_experts):
    b, s, d = x_new.shape
    tile, row_chunk = _moe_tile(s)
    n_chunks = N_EXPERTS // chunk_experts
    cf = chunk_experts * EXPERT_FF
    tok = lambda bb, j, e: (bb, j, 0)
    kern = functools.partial(_moe_kernel, row_chunk=row_chunk, ctx_rows=ctx_rows)
    return pl.pallas_call(
        kern,
        grid=(b, s // tile, n_chunks),
        in_specs=[
            pl.BlockSpec((None, tile, d), tok),
            pl.BlockSpec((None, tile, d), tok),
            pl.BlockSpec((None, None, tile, chunk_experts), lambda bb, j, e: (e, bb, j, 0)),
            pl.BlockSpec((None, None, 2, N_MOD, d), lambda bb, j, e: (i, bb, 0, 0, 0)),
            pl.BlockSpec((None, d, cf), lambda bb, j, e: (i, 0, e)),
            pl.BlockSpec((None, d, cf), lambda bb, j, e: (i, 0, e)),
            pl.BlockSpec((None, cf, d), lambda bb, j, e: (i, e, 0)),
        ],
        out_specs=pl.BlockSpec((None, tile, d), tok),
        out_shape=jax.ShapeDtypeStruct((b, s, d), F32),
        input_output_aliases={0: 0},
        compiler_params=pltpu.CompilerParams(
            dimension_semantics=("arbitrary", "arbitrary", "arbitrary"), vmem_limit_bytes=VMEM_LIMIT),
    )(x_new, h2, cmb, mod, lw["w1"], lw["w3"], lw["w2"])


def _prepare(p):
    dpt = p["w_in"].shape[0]
    w_in = p["w_in"]
    zeros = lambda n: jnp.zeros((dpt, D_MODEL, n), F32)
    w_a = jnp.concatenate([
        w_in[..., KV_OFF:KR_OFF], zeros(NOPE_DIM), w_in[..., KR_OFF:Q_OFF],
        zeros(LANES - QK_DIM), w_in[..., Q_OFF:CONV_OFF]], axis=-1)
    gb_lo, gc_lo, u_lo = CONV_OFF, CONV_OFF + BRANCH_W, CONV_OFF + 2 * BRANCH_W
    w_h = jnp.concatenate([w_in[..., gc_lo:u_lo], w_in[..., u_lo:POOL_OFF],
                           w_in[..., POOL_OFF:SGU_OFF]], axis=-1)
    w_m = jnp.concatenate([w_in[..., gb_lo:gc_lo], w_in[..., SGU_OFF:]], axis=-1)
    w_ukv = p["w_ukv"]
    w_uk = jnp.concatenate([w_ukv[..., :NOPE_DIM], jnp.zeros_like(w_ukv[..., :NOPE_DIM])], axis=-1)
    w_uv = w_ukv[..., NOPE_DIM:]
    w_uq = jnp.pad(p["w_uq"], ((0, 0), (0, 0), (0, 0), (0, LANES - QK_DIM)))
    pad_g = lambda g: jnp.pad(g, ((0, 0), (0, LANES - QK_DIM)))[:, None, :]
    eye = jnp.eye(TOKEN_TILE // SGU_CHUNK, dtype=F32)
    sgu_bd = jnp.einsum("ab,lgpq->lgapbq", eye, p["sgu_w"]).reshape(
        dpt, SGU_G, TOKEN_TILE, TOKEN_TILE)
    sgu_b = jnp.repeat(jnp.swapaxes(p["sgu_b"], 1, 2), SGU_CHUNK, axis=2)
    sgu_b = jnp.tile(sgu_b, (1, TOKEN_TILE // SGU_CHUNK, 1))
    moe_cols = lambda w: jnp.swapaxes(w, 1, 2).reshape(dpt, D_MODEL, N_EXPERTS * EXPERT_FF)
    return {
        "g1": p["norm1_g"][:, None, :], "g2": p["norm2_g"][:, None, :],
        "w_a": w_a.astype(BF16), "w_h": w_h.astype(BF16), "w_m": w_m.astype(BF16),
        "g_kv": p["kv_a_norm_g"][:, None, :], "g_qa": p["q_a_norm_g"][:, None, :],
        "w_uk": w_uk.reshape(dpt, KV_LORA, N_HEADS * LANES).astype(BF16),
        "w_uv": w_uv.reshape(dpt, KV_LORA, N_HEADS * V_DIM).astype(BF16),
        "w_uq": w_uq.reshape(dpt, Q_LORA, N_HEADS * LANES).astype(BF16),
        "g_k": pad_g(p["k_norm_g"]), "g_q": pad_g(p["q_norm_g"]),
        "conv_w": p["conv_w"], "pool_w": p["pool_w"].astype(BF16),
        "pool_scale": p["pool_scale"][:, None, :], "g_sgu": p["sgu_norm_g"][:, None, :],
        "sgu_w": sgu_bd.astype(BF16), "sgu_b": sgu_b,
        "w_branch": p["w_branch"].astype(BF16), "w_o": p["w_o"].astype(BF16),
        "w1": moe_cols(p["moe_w1"]).astype(BF16), "w3": moe_cols(p["moe_w3"]).astype(BF16),
        "w2": p["moe_w2"].reshape(dpt, N_EXPERTS * EXPERT_FF, D_MODEL).astype(BF16),
    }


def _rope_tables(ctx_len, length):
    rows = length // GRID_W
    t = jnp.arange(rows * GRID_W)
    grid = jnp.stack([t // GRID_W, t % GRID_W], -1).astype(F32)
    inv_freq = ROPE_THETA ** (-jnp.arange(AXIS_PAIRS, dtype=F32) / AXIS_PAIRS)
    ang = jnp.concatenate([grid[:, :1] * inv_freq, grid[:, 1:] * inv_freq], -1)
    cos, sin = jnp.cos(ang), jnp.sin(ang)
    one = jnp.ones((length, NOPE_DIM), F32)
    tail = jnp.ones((length, LANES - QK_DIM), F32)
    zn = jnp.zeros((length, NOPE_DIM), F32)
    zh = jnp.zeros((length, ROPE_HALF), F32)
    zt = jnp.zeros((length, LANES - QK_DIM), F32)
    rc = jnp.concatenate([one, cos, cos, tail], -1)
    rsa = jnp.concatenate([zn, -sin, zh, zt], -1)
    rsb = jnp.concatenate([zn, zh, sin, zt], -1)
    ctx_id = lambda fill: jnp.full((ctx_len, LANES), fill, F32)
    return (jnp.concatenate([ctx_id(1.0), rc], 0), jnp.concatenate([ctx_id(0.0), rsa], 0),
            jnp.concatenate([ctx_id(0.0), rsb], 0))


def kernel(x, c, ctx, c_ctx, mod_w, mod_b, norm1_g, norm2_g, w_in, kv_a_norm_g, w_ukv, k_norm_g,
           q_a_norm_g, w_uq, q_norm_g, conv_w, pool_w, pool_scale, sgu_norm_g, sgu_w, sgu_b,
           w_branch, w_o, router_w, router_b, moe_w1, moe_w3, moe_w2):
    b, length, d = x.shape
    ctx_len = ctx.shape[1]
    assert d == D_MODEL and length % TOKEN_TILE == 0 and ctx_len % TOKEN_TILE == 0
    assert length % GRID_W == 0
    n_ctx_tiles = ctx_len // TOKEN_TILE
    s = ctx_len + length

    lw = _prepare(dict(
        norm1_g=norm1_g, norm2_g=norm2_g, w_in=w_in, kv_a_norm_g=kv_a_norm_g, w_ukv=w_ukv,
        k_norm_g=k_norm_g, q_a_norm_g=q_a_norm_g, w_uq=w_uq, q_norm_g=q_norm_g, conv_w=conv_w,
        pool_w=pool_w, pool_scale=pool_scale, sgu_norm_g=sgu_norm_g, sgu_w=sgu_w, sgu_b=sgu_b,
        w_branch=w_branch, w_o=w_o, moe_w1=moe_w1, moe_w3=moe_w3, moe_w2=moe_w2))
    rope_tabs = _rope_tables(ctx_len, length)
    rw = jnp.pad(router_w, ((0, 0), (0, LANES - N_EXPERTS)))
    rb = jnp.pad(router_b, (0, LANES - N_EXPERTS))[None, :]

    rows = -(-(b + 1) // 8) * 8
    cvec = jnp.concatenate([c, c_ctx[None, :], jnp.zeros((rows - b - 1, d), F32)], 0)
    mod = _modulation(cvec, mod_w, mod_b).reshape(DEPTH, rows, N_MOD, d)
    mod = jnp.stack([jnp.broadcast_to(mod[:, b:b + 1], (DEPTH, b, N_MOD, d)), mod[:, :b]], axis=2)

    xs = jnp.concatenate([ctx, x], axis=1)
    chunk_experts = 4
    for i in range(DEPTH):
        last = i == DEPTH - 1
        first = n_ctx_tiles if last else 0
        q, k, v = _qkv(xs, mod, lw, i, rope_tabs, n_ctx_tiles)
        attn = _attention(q, k, v, first, n_ctx_tiles)
        x_new, h2, cmb = _mix(xs, mod, attn, lw, i, first, n_ctx_tiles, rw, rb)
        cmb = cmb[..., :N_EXPERTS].reshape(b, -1, N_EXPERTS // chunk_experts, chunk_experts)
        cmb = jnp.moveaxis(cmb, 2, 0)
        xs = _moe(x_new, h2, cmb, mod, lw, i, 0 if last else ctx_len, chunk_experts)
    return xs
```

```python
import functools

import jax
import jax.numpy as jnp
from jax import lax
from jax.experimental import pallas as pl
from jax.experimental.pallas import tpu as pltpu

F32 = jnp.float32
BF16 = jnp.bfloat16

D_MODEL = 1024
DEPTH = 4
GRID_W = 64
EPS = 1e-6
N_MOD = 6
N_BRANCH = 4
BRANCH_W = 512
N_HEADS = 8
Q_LORA = 256
KV_LORA = 128
NOPE_DIM = 64
ROPE_DIM = 32
V_DIM = 64
QK_DIM = NOPE_DIM + ROPE_DIM
ROPE_HALF = ROPE_DIM // 2
AXIS_PAIRS = ROPE_DIM // 4
ROPE_THETA = 10000.0
ATTN_SCALE = QK_DIM ** -0.5
LOG2E = 1.4426950408889634
CONV_K = 3
POOL_WINDOWS = (2, 4, 8, 16)
POOL_GC = 128
SGU_CHUNK = 128
SGU_G = 4
N_EXPERTS = 16
N_EXPERT_GROUPS = 4
EXPERTS_PER_GROUP = 4
EXPERT_FF = 256

KV_OFF = 0
KR_OFF = KV_OFF + KV_LORA
Q_OFF = KR_OFF + ROPE_DIM
CONV_OFF = Q_OFF + Q_LORA
POOL_OFF = CONV_OFF + 3 * BRANCH_W
SGU_OFF = POOL_OFF + BRANCH_W
GATE_OFF = SGU_OFF + 2 * BRANCH_W

LANES = 128
HALO = 16
TOKEN_TILE = 256
VMEM_LIMIT = 56 * 1024 * 1024


def _rms(x, width=None):
    width = x.shape[-1] if width is None else width
    return x * lax.rsqrt(jnp.sum(x * x, axis=-1, keepdims=True) * (1.0 / width) + EPS)


def _norm_mod(x, g, shift, scale):
    return (_rms(x) * g) * (1.0 + scale) + shift


def _dot(a, b):
    return jnp.dot(a, b, preferred_element_type=F32)


def _const_spec(shape, index_map):
    return pl.BlockSpec(shape, index_map, pipeline_mode=pl.Buffered(1))


def _mod_kernel(c_ref, w_ref, b_ref, o_ref):
    c = c_ref[...]
    sc = c * jax.nn.sigmoid(c)
    o_ref[...] = jnp.dot(sc, w_ref[...], preferred_element_type=F32,
                         precision=lax.Precision.HIGHEST) + b_ref[...]


def _modulation(cvec, mod_w, mod_b):
    rows = cvec.shape[0]
    n = mod_w.shape[-1]
    bn = 1536
    return pl.pallas_call(
        _mod_kernel,
        grid=(DEPTH, n // bn),
        in_specs=[
            pl.BlockSpec((rows, D_MODEL), lambda i, j: (0, 0)),
            pl.BlockSpec((None, D_MODEL, bn), lambda i, j: (i, 0, j)),
            pl.BlockSpec((None, 1, bn), lambda i, j: (i, 0, j)),
        ],
        out_specs=pl.BlockSpec((None, rows, bn), lambda i, j: (i, 0, j)),
        out_shape=jax.ShapeDtypeStruct((DEPTH, rows, n), F32),
        compiler_params=pltpu.CompilerParams(
            dimension_semantics=("arbitrary", "arbitrary"), vmem_limit_bytes=VMEM_LIMIT),
    )(cvec, mod_w, mod_b.reshape(DEPTH, 1, n))


def _qkv_kernel(x_ref, mod_ref, g1_ref, wa_ref, gkv_ref, gq_ref, wuk_ref, wuv_ref, wuq_ref, wuqs_ref,
                gak_ref, gbk_ref, gaq_ref, gbq_ref, q_ref, k_ref, v_ref):
    x = x_ref[...]
    m = mod_ref[...]
    h = _norm_mod(x, g1_ref[...], m[0:1], m[1:2]).astype(BF16)
    p = _dot(h, wa_ref[...])
    ckv = (_rms(p[:, :KV_LORA]) * gkv_ref[...]).astype(BF16)
    kk = _dot(ckv, wuk_ref[...])
    vv = _dot(ckv, wuv_ref[...])
    kr = p[:, LANES:2 * LANES]
    krs = p[:, 2 * LANES:3 * LANES]
    cq = (_rms(p[:, 3 * LANES:]) * gq_ref[...]).astype(BF16)
    qq = _dot(cq, wuq_ref[...])
    qs = _dot(cq, wuqs_ref[...])
    gak, gbk, gaq, gbq = gak_ref[...], gbk_ref[...], gaq_ref[...], gbq_ref[...]
    k_shared = kr * gak + krs * gbk
    kr_sq = jnp.sum(kr * kr, axis=-1, keepdims=True)
    lane = lax.broadcasted_iota(jnp.int32, (1, LANES), 1)
    one_col = [jnp.where(lane == V_DIM, 1.0, 0.0), jnp.where(lane == 0, 1.0, 0.0)]

    for hd in range(N_HEADS):
        cols = slice(hd * LANES, (hd + 1) * LANES)
        kn = kk[:, cols]
        rk = lax.rsqrt((jnp.sum(kn * kn, axis=-1, keepdims=True) + kr_sq) * (1.0 / QK_DIM) + EPS)
        k_ref[hd] = ((kn * gak + k_shared) * rk).astype(BF16)
        qh = qq[:, cols]
        rq = lax.rsqrt(jnp.sum(qh * qh, axis=-1, keepdims=True) * (1.0 / QK_DIM) + EPS)
        q_ref[hd] = ((qh * gaq + qs[:, cols] * gbq) * rq).astype(BF16)
        v_ref[hd] = (vv[:, cols] + one_col[hd % 2]).astype(BF16)


def _qkv(xs, mod, lw, i, rope_tabs, n_ctx_tiles):
    b, s, d = xs.shape
    tm = TOKEN_TILE
    nt = s // tm

    def cw(shape):
        return _const_spec((None,) + shape, lambda bb, j: (i,) + (0,) * len(shape))

    tok = lambda bb, j: (bb, j, 0)
    head = lambda bb, j: (bb, 0, j, 0)
    rspec = pl.BlockSpec((None, tm, LANES), lambda bb, j: (i, j, 0))
    hspec = pl.BlockSpec((None, N_HEADS, tm, LANES), head)
    hshape = jax.ShapeDtypeStruct((b, N_HEADS, s, LANES), BF16)
    return pl.pallas_call(
        _qkv_kernel,
        grid=(b, nt),
        in_specs=[
            pl.BlockSpec((None, tm, d), tok),
            pl.BlockSpec((None, None, None, N_MOD, d),
                         lambda bb, j: (i, bb, jnp.where(j < n_ctx_tiles, 0, 1), 0, 0)),
            cw((1, d)), cw((d, 5 * LANES)), cw((1, KV_LORA)), cw((1, Q_LORA)),
            cw((KV_LORA, N_HEADS * LANES)), cw((KV_LORA, N_HEADS * LANES)),
            cw((Q_LORA, N_HEADS * LANES)), cw((Q_LORA, N_HEADS * LANES)),
            rspec, rspec, rspec, rspec,
        ],
        out_specs=[hspec, hspec, hspec],
        out_shape=[hshape, hshape, hshape],
        compiler_params=pltpu.CompilerParams(
            dimension_semantics=("arbitrary", "arbitrary"), vmem_limit_bytes=VMEM_LIMIT),
    )(xs, mod, lw["g1"], lw["w_a"], lw["g_kv"], lw["g_qa"], lw["w_uk"], lw["w_uv"], lw["w_uq"],
      lw["w_uqs"], *rope_tabs)


def _attn_kernel(q_ref, k_ref, v_ref, o_ref, *, first_tile, n_ctx_tiles, ctx_len):
    n_keys = k_ref.shape[1]
    lane = lax.broadcasted_iota(jnp.int32, (1, LANES), 1)

    def attend(nk):
        for pair in range(N_HEADS // 2):
            outs = []
            for hd in (2 * pair, 2 * pair + 1):
                s = lax.dot_general(q_ref[hd], k_ref[hd, 0:nk, :], (((1,), (1,)), ((), ())),
                                    preferred_element_type=F32)
                e = jnp.exp2(s - jnp.max(s, axis=-1, keepdims=True))
                o = _dot(e.astype(BF16), v_ref[hd, 0:nk, :])
                den = o[:, V_DIM:V_DIM + 1] if hd % 2 == 0 else o[:, 0:1]
                outs.append(o / den)
            o_ref[:, pair * LANES:(pair + 1) * LANES] = jnp.where(
                lane < V_DIM, outs[0], outs[1]).astype(BF16)

    if first_tile >= n_ctx_tiles:
        attend(n_keys)
    else:
        j = pl.program_id(1) + first_tile
        pl.when(j < n_ctx_tiles)(lambda: attend(ctx_len))
        pl.when(j >= n_ctx_tiles)(lambda: attend(n_keys))


def _attention(q, k, v, first_tile, n_ctx_tiles):
    b, _, s, _ = q.shape
    tm = TOKEN_TILE
    nt = s // tm
    kern = functools.partial(_attn_kernel, first_tile=first_tile, n_ctx_tiles=n_ctx_tiles,
                             ctx_len=n_ctx_tiles * tm)
    return pl.pallas_call(
        kern,
        grid=(b, nt - first_tile),
        in_specs=[
            pl.BlockSpec((None, N_HEADS, tm, LANES), lambda bb, j: (bb, 0, j + first_tile, 0)),
            pl.BlockSpec((None, N_HEADS, s, LANES), lambda bb, j: (bb, 0, 0, 0)),
            pl.BlockSpec((None, N_HEADS, s, LANES), lambda bb, j: (bb, 0, 0, 0)),
        ],
        out_specs=pl.BlockSpec((None, tm, BRANCH_W), lambda bb, j: (bb, j + first_tile, 0)),
        out_shape=jax.ShapeDtypeStruct((b, s, BRANCH_W), BF16),
        compiler_params=pltpu.CompilerParams(
            dimension_semantics=("arbitrary", "arbitrary"), vmem_limit_bytes=VMEM_LIMIT),
    )(q, k, v)


def _route(logits, rbias):
    lane = lax.broadcasted_iota(jnp.int32, logits.shape, 1)
    real = lane < N_EXPERTS
    slot = lane % EXPERTS_PER_GROUP
    s = jax.nn.sigmoid(logits)
    neg = jnp.float32(-jnp.inf)
    sb = jnp.where(real, s + rbias, neg)

    def mate(x, dd):
        fwd = pltpu.roll(x, LANES - dd, 1)
        back = pltpu.roll(x, EXPERTS_PER_GROUP - dd, 1)
        return jnp.where(slot + dd < EXPERTS_PER_GROUP, fwd, back)

    mates = [mate(sb, dd) for dd in (1, 2, 3)]
    vals = [sb] + mates
    gscore = None
    for a in range(4):
        for c in range(a + 1, 4):
            pair = vals[a] + vals[c]
            gscore = pair if gscore is None else jnp.maximum(gscore, pair)
    gscore = jnp.where(real, gscore, neg)
    group_ok = real
    for gg in (1, 2, 3):
        lower = pltpu.roll(gscore, EXPERTS_PER_GROUP * gg, 1)
        higher = pltpu.roll(gscore, LANES - EXPERTS_PER_GROUP * gg, 1)
        group_ok = jnp.logical_and(group_ok, jnp.logical_and(gscore > lower, gscore >= higher))
    rank = jnp.zeros(logits.shape, jnp.int32)
    for dd, mv in zip((1, 2, 3), mates):
        wrapped = slot + dd >= EXPERTS_PER_GROUP
        beats = jnp.logical_or(mv > sb, jnp.logical_and(wrapped, mv == sb))
        rank = rank + beats.astype(jnp.int32)
    chosen = jnp.logical_and(rank < 2, group_ok)
    w = jnp.where(chosen, s, 0.0)
    return w / jnp.sum(w, axis=-1, keepdims=True)


def _mix_kernel(x_ref, xp_ref, xn_ref, mod_ref, g1_ref, g2_ref, wh_ref, wm_ref, attn_ref,
                convw_ref, poolw_ref, pscale_ref, gsgu_ref, sguw_ref, sgub_ref, wbr_ref, wo_ref,
                rw_ref, rb_ref, xo_ref, h2_ref, cmb_ref, hcat_ref,
                *, first_tile, n_ctx_tiles, n_tiles):
    tm = x_ref.shape[0]
    j = pl.program_id(1) + first_tile
    m = mod_ref[...]
    g1 = g1_ref[...]
    x = x_ref[...]

    def hnorm(xv):
        return _norm_mod(xv, g1, m[0:1], m[1:2]).astype(BF16)

    hm = hnorm(x)
    hcat_ref[0:HALO, :] = hnorm(xp_ref[...])
    hcat_ref[HALO:HALO + tm, :] = hm
    hcat_ref[HALO + tm:, :] = hnorm(xn_ref[...])

    in_ctx = j < n_ctx_tiles
    seg_first = jnp.where(in_ctx, 0, n_ctx_tiles)
    seg_last = jnp.where(in_ctx, n_ctx_tiles - 1, n_tiles - 1)
    seg_len = (seg_last - seg_first + 1) * tm
    row = lax.broadcasted_iota(jnp.int32, (tm + 2 * HALO, 1), 0)
    keep = jnp.logical_and(jnp.logical_or(row >= HALO, j > seg_first),
                           jnp.logical_or(row < HALO + tm, j < seg_last))

    ph = _dot(hcat_ref[...], wh_ref[...])
    ph = jnp.where(keep, ph, 0.0)
    nrow = tm + 2 * HALO

    def shifted(a, k):
        return pltpu.roll(a, (-k) % nrow, 0)[HALO:HALO + tm]

    cu = ph[:, :BRANCH_W] * ph[:, BRANCH_W:2 * BRANCH_W]
    cw = convw_ref[...]
    conv = shifted(cu, -1) * cw[0:1] + cu[HALO:HALO + tm] * cw[1:2] + shifted(cu, 1) * cw[2:3]
    gb = _dot(hm, wm_ref[:, 0:BRANCH_W])
    br_conv = gb * conv

    pp = ph[:, 2 * BRANCH_W:]
    pos = (j - seg_first) * tm + lax.broadcasted_iota(jnp.int32, (tm, 1), 0)
    level = pp + pltpu.roll(pp, 1, 0)
    pooled = []
    for gi, win in enumerate(POOL_WINDOWS):
        if gi > 0:
            half = win // 4
            level = level[:, POOL_GC:]
            level = pltpu.roll(level, nrow - half, 0) + pltpu.roll(level, half, 0)
        cnt = (jnp.minimum(pos + win // 2, seg_len) - jnp.maximum(pos - win // 2, 0)).astype(F32)
        own = pp[HALO:HALO + tm, gi * POOL_GC:(gi + 1) * POOL_GC]
        pooled.append(level[HALO:HALO + tm, :POOL_GC] / cnt - own)
    br_pool = jnp.concatenate(
        [_dot(pooled[gi].astype(BF16), poolw_ref[gi]) for gi in range(len(POOL_WINDOWS))], axis=-1)
    br_pool = br_pool * pscale_ref[...]

    z = jax.nn.gelu(_dot(hm, wm_ref[:, BRANCH_W:3 * BRANCH_W]), approximate=True)
    zu = z[:, :BRANCH_W]
    zv = (_rms(z[:, BRANCH_W:]) * gsgu_ref[...]).astype(BF16)
    mixed = jnp.concatenate(
        [_dot(sguw_ref[gi], zv[:, gi * LANES:(gi + 1) * LANES]) for gi in range(SGU_G)], axis=-1)
    br_sgu = zu * (mixed + sgub_ref[...])

    branches = (attn_ref[...], br_conv.astype(BF16), br_pool.astype(BF16), br_sgu.astype(BF16))
    merged = None
    for nb, br in enumerate(branches):
        lo = 3 * BRANCH_W + nb * D_MODEL
        gate = jax.nn.sigmoid(_dot(hm, wm_ref[:, lo:lo + D_MODEL]))
        term = gate * _dot(br, wbr_ref[nb])
        merged = term if merged is None else merged + term
    x_new = x + m[2:3] * _dot(merged.astype(BF16), wo_ref[...])
    xo_ref[...] = x_new

    h2 = _norm_mod(x_new, g2_ref[...], m[3:4], m[4:5])
    h2_ref[...] = h2.astype(BF16)
    logits = jnp.dot(h2, rw_ref[...], preferred_element_type=F32, precision=lax.Precision.HIGHEST)
    cmb_ref[...] = _route(logits, rb_ref[...])


def _mix(xs, mod, attn, lw, i, first_tile, n_ctx_tiles, router_w, router_b):
    b, s, d = xs.shape
    tm = TOKEN_TILE
    nt = s // tm
    hb = tm // HALO
    n_hblk = s // HALO

    def cw(shape):
        return _const_spec((None,) + shape, lambda bb, j: (i,) + (0,) * len(shape))

    tok = lambda bb, j: (bb, j + first_tile, 0)
    out_tok = lambda bb, j: (bb, j, 0)
    s_out = s - first_tile * tm
    kern = functools.partial(_mix_kernel, first_tile=first_tile, n_ctx_tiles=n_ctx_tiles, n_tiles=nt)
    return pl.pallas_call(
        kern,
        grid=(b, nt - first_tile),
        in_specs=[
            pl.BlockSpec((None, tm, d), tok),
            pl.BlockSpec((None, HALO, d),
                         lambda bb, j: (bb, jnp.maximum((j + first_tile) * hb - 1, 0), 0)),
            pl.BlockSpec((None, HALO, d),
                         lambda bb, j: (bb, jnp.minimum((j + first_tile + 1) * hb, n_hblk - 1), 0)),
            pl.BlockSpec((None, None, None, N_MOD, d),
                         lambda bb, j: (i, bb, jnp.where(j + first_tile < n_ctx_tiles, 0, 1), 0, 0)),
            cw((1, d)), cw((1, d)),
            cw((d, 3 * BRANCH_W)), cw((d, 3 * BRANCH_W + N_BRANCH * D_MODEL)),
            pl.BlockSpec((None, tm, BRANCH_W), tok),
            cw((CONV_K, BRANCH_W)), cw((len(POOL_WINDOWS), POOL_GC, POOL_GC)), cw((1, BRANCH_W)),
            cw((1, BRANCH_W)), cw((SGU_G, tm, tm)), cw((tm, BRANCH_W)),
            cw((N_BRANCH, BRANCH_W, D_MODEL)), cw((D_MODEL, D_MODEL)),
            _const_spec((d, LANES), lambda bb, j: (0, 0)),
            _const_spec((1, LANES), lambda bb, j: (0, 0)),
        ],
        out_specs=[
            pl.BlockSpec((None, tm, d), out_tok),
            pl.BlockSpec((None, tm, d), out_tok),
            pl.BlockSpec((None, tm, LANES), out_tok),
        ],
        out_shape=[
            jax.ShapeDtypeStruct((b, s_out, d), F32),
            jax.ShapeDtypeStruct((b, s_out, d), BF16),
            jax.ShapeDtypeStruct((b, s_out, LANES), F32),
        ],
        scratch_shapes=[pltpu.VMEM((tm + 2 * HALO, d), BF16)],
        compiler_params=pltpu.CompilerParams(
            dimension_semantics=("arbitrary", "arbitrary"), vmem_limit_bytes=VMEM_LIMIT),
    )(xs, xs, xs, mod, lw["g1"], lw["g2"], lw["w_h"], lw["w_m"], attn,
      lw["conv_w"], lw["pool_w"], lw["pool_scale"], lw["g_sgu"], lw["sgu_w"], lw["sgu_b"],
      lw["w_branch"], lw["w_o"], router_w, router_b)


def _moe_kernel(x_ref, h_ref, cmb_ref, mod_ref, w1_ref, w3_ref, w2_ref, o_ref, *, row_chunk, ctx_rows):
    tile = x_ref.shape[0]
    ec = pl.program_id(2)
    n_exp = w1_ref.shape[-1] // EXPERT_FF

    @pl.when(ec == 0)
    def _():
        o_ref[...] = x_ref[...]

    for r in range(tile // row_chunk):
        rows = pl.ds(r * row_chunk, row_chunk)
        gate = mod_ref[1, N_MOD - 1:N_MOD]
        if ctx_rows:
            pos = (pl.program_id(1) * tile + r * row_chunk
                   + lax.broadcasted_iota(jnp.int32, (row_chunk, 1), 0))
            gate = jnp.where(pos < ctx_rows, mod_ref[0, N_MOD - 1:N_MOD], gate)
        h = h_ref[rows, :]
        a = _dot(h, w1_ref[...])
        a = (a * jax.nn.sigmoid(a)) * _dot(h, w3_ref[...])
        cmb = cmb_ref[rows, :]
        a = jnp.concatenate(
            [a[:, e * EXPERT_FF:(e + 1) * EXPERT_FF] * cmb[:, e:e + 1] for e in range(n_exp)], axis=-1)
        o_ref[rows, :] += gate * _dot(a.astype(BF16), w2_ref[...])


def _moe_tile(s):
    for tile, chunk in ((1152, 384), (1024, 512), (768, 384), (512, 256), (256, 256)):
        if s % tile == 0:
            return tile, chunk
    raise ValueError(f"unsupported sequence length {s}")


def _moe(x_new, h2, cmb, mod, lw, i, ctx_rows, chunk_experts):
    b, s, d = x_new.shape
    tile, row_chunk = _moe_tile(s)
    n_chunks = N_EXPERTS // chunk_experts
    cf = chunk_experts * EXPERT_FF
    tok = lambda bb, j, e: (bb, j, 0)
    kern = functools.partial(_moe_kernel, row_chunk=row_chunk, ctx_rows=ctx_rows)
    return pl.pallas_call(
        kern,
        grid=(b, s // tile, n_chunks),
        in_specs=[
            pl.BlockSpec((None, tile, d), tok),
            pl.BlockSpec((None, tile, d), tok),
            pl.BlockSpec((None, None, tile, chunk_experts), lambda bb, j, e: (e, bb, j, 0)),
            pl.BlockSpec((None, None, 2, N_MOD, d), lambda bb, j, e: (i, bb, 0, 0, 0)),
            pl.BlockSpec((None, d, cf), lambda bb, j, e: (i, 0, e)),
            pl.BlockSpec((None, d, cf), lambda bb, j, e: (i, 0, e)),
            pl.BlockSpec((None, cf, d), lambda bb, j, e: (i, e, 0)),
        ],
        out_specs=pl.BlockSpec((None, tile, d), tok),
        out_shape=jax.ShapeDtypeStruct((b, s, d), F32),
        input_output_aliases={0: 0},
        compiler_params=pltpu.CompilerParams(
            dimension_semantics=("arbitrary", "arbitrary", "arbitrary"), vmem_limit_bytes=VMEM_LIMIT),
    )(x_new, h2, cmb, mod, lw["w1"], lw["w3"], lw["w2"])


def _prepare(p):
    dpt = p["w_in"].shape[0]
    w_in = p["w_in"]
    zeros = lambda n: jnp.zeros((dpt, D_MODEL, n), F32)
    kr_mid = KR_OFF + ROPE_HALF
    w_a = jnp.concatenate([
        w_in[..., KV_OFF:KR_OFF],
        zeros(NOPE_DIM), w_in[..., KR_OFF:Q_OFF], zeros(LANES - QK_DIM),
        zeros(NOPE_DIM), w_in[..., kr_mid:Q_OFF], w_in[..., KR_OFF:kr_mid], zeros(LANES - QK_DIM),
        w_in[..., Q_OFF:CONV_OFF]], axis=-1)
    gb_lo, gc_lo, u_lo = CONV_OFF, CONV_OFF + BRANCH_W, CONV_OFF + 2 * BRANCH_W
    w_h = jnp.concatenate([w_in[..., gc_lo:u_lo], w_in[..., u_lo:POOL_OFF],
                           w_in[..., POOL_OFF:SGU_OFF]], axis=-1)
    w_m = jnp.concatenate([w_in[..., gb_lo:gc_lo], w_in[..., SGU_OFF:]], axis=-1)
    w_ukv = p["w_ukv"]
    w_uk = jnp.concatenate([w_ukv[..., :NOPE_DIM], jnp.zeros_like(w_ukv[..., :NOPE_DIM])], axis=-1)
    w_v = w_ukv[..., NOPE_DIM:]
    even = (jnp.arange(N_HEADS) % 2 == 0)[None, None, :, None]
    w_uv = jnp.concatenate([jnp.where(even, w_v, 0.0), jnp.where(even, 0.0, w_v)], axis=-1)
    w_q = p["w_uq"]
    w_uq = jnp.pad(w_q, ((0, 0), (0, 0), (0, 0), (0, LANES - QK_DIM)))
    rope_mid = NOPE_DIM + ROPE_HALF
    w_uqs = jnp.concatenate([
        jnp.zeros_like(w_q[..., :NOPE_DIM]), w_q[..., rope_mid:], w_q[..., NOPE_DIM:rope_mid],
        jnp.zeros_like(w_q[..., :LANES - QK_DIM])], axis=-1)
    eye = jnp.eye(TOKEN_TILE // SGU_CHUNK, dtype=F32)
    sgu_bd = jnp.einsum("ab,lgpq->lgapbq", eye, p["sgu_w"]).reshape(
        dpt, SGU_G, TOKEN_TILE, TOKEN_TILE)
    sgu_b = jnp.repeat(jnp.swapaxes(p["sgu_b"], 1, 2), SGU_CHUNK, axis=2)
    sgu_b = jnp.tile(sgu_b, (1, TOKEN_TILE // SGU_CHUNK, 1))
    moe_cols = lambda w: jnp.swapaxes(w, 1, 2).reshape(dpt, D_MODEL, N_EXPERTS * EXPERT_FF)
    return {
        "g1": p["norm1_g"][:, None, :], "g2": p["norm2_g"][:, None, :],
        "w_a": w_a.astype(BF16), "w_h": w_h.astype(BF16), "w_m": w_m.astype(BF16),
        "g_kv": p["kv_a_norm_g"][:, None, :], "g_qa": p["q_a_norm_g"][:, None, :],
        "w_uk": w_uk.reshape(dpt, KV_LORA, N_HEADS * LANES).astype(BF16),
        "w_uv": w_uv.reshape(dpt, KV_LORA, N_HEADS * LANES).astype(BF16),
        "w_uq": w_uq.reshape(dpt, Q_LORA, N_HEADS * LANES).astype(BF16),
        "w_uqs": w_uqs.reshape(dpt, Q_LORA, N_HEADS * LANES).astype(BF16),
        "conv_w": p["conv_w"], "pool_w": p["pool_w"].astype(BF16),
        "pool_scale": p["pool_scale"][:, None, :], "g_sgu": p["sgu_norm_g"][:, None, :],
        "sgu_w": sgu_bd.astype(BF16), "sgu_b": sgu_b,
        "w_branch": p["w_branch"].astype(BF16), "w_o": p["w_o"].astype(BF16),
        "w1": moe_cols(p["moe_w1"]).astype(BF16), "w3": moe_cols(p["moe_w3"]).astype(BF16),
        "w2": p["moe_w2"].reshape(dpt, N_EXPERTS * EXPERT_FF, D_MODEL).astype(BF16),
    }


def _rope_tables(ctx_len, length, k_gain, q_gain):
    rows = length // GRID_W
    t = jnp.arange(rows * GRID_W)
    grid = jnp.stack([t // GRID_W, t % GRID_W], -1).astype(F32)
    inv_freq = ROPE_THETA ** (-jnp.arange(AXIS_PAIRS, dtype=F32) / AXIS_PAIRS)
    ang = jnp.concatenate([grid[:, :1] * inv_freq, grid[:, 1:] * inv_freq], -1)
    cos, sin = jnp.cos(ang), jnp.sin(ang)
    one = jnp.ones((length, NOPE_DIM), F32)
    tail = jnp.ones((length, LANES - QK_DIM), F32)
    zn = jnp.zeros((length, NOPE_DIM), F32)
    zt = jnp.zeros((length, LANES - QK_DIM), F32)
    rc = jnp.concatenate([one, cos, cos, tail], -1)
    rs = jnp.concatenate([zn, -sin, sin, zt], -1)
    ctx_id = lambda fill: jnp.full((ctx_len, LANES), fill, F32)
    rc = jnp.concatenate([ctx_id(1.0), rc], 0)
    rs = jnp.concatenate([ctx_id(0.0), rs], 0)

    def tables(gain, const):
        g = jnp.pad(gain, ((0, 0), (0, LANES - QK_DIM)))
        mid = NOPE_DIM + ROPE_HALF
        g_swap = jnp.concatenate([g[:, :NOPE_DIM], g[:, mid:QK_DIM], g[:, NOPE_DIM:mid],
                                  g[:, QK_DIM:]], -1)
        return (g[:, None, :] * rc[None]) * const, (g_swap[:, None, :] * rs[None]) * const

    gak, gbk = tables(k_gain, 1.0)
    gaq, gbq = tables(q_gain, ATTN_SCALE * LOG2E)
    return gak, gbk, gaq, gbq


def kernel(x, c, ctx, c_ctx, mod_w, mod_b, norm1_g, norm2_g, w_in, kv_a_norm_g, w_ukv, k_norm_g,
           q_a_norm_g, w_uq, q_norm_g, conv_w, pool_w, pool_scale, sgu_norm_g, sgu_w, sgu_b,
           w_branch, w_o, router_w, router_b, moe_w1, moe_w3, moe_w2):
    b, length, d = x.shape
    ctx_len = ctx.shape[1]
    assert d == D_MODEL and length % TOKEN_TILE == 0 and ctx_len % TOKEN_TILE == 0
    assert length % GRID_W == 0
    n_ctx_tiles = ctx_len // TOKEN_TILE
    s = ctx_len + length

    lw = _prepare(dict(
        norm1_g=norm1_g, norm2_g=norm2_g, w_in=w_in, kv_a_norm_g=kv_a_norm_g, w_ukv=w_ukv,
        k_norm_g=k_norm_g, q_a_norm_g=q_a_norm_g, w_uq=w_uq, q_norm_g=q_norm_g, conv_w=conv_w,
        pool_w=pool_w, pool_scale=pool_scale, sgu_norm_g=sgu_norm_g, sgu_w=sgu_w, sgu_b=sgu_b,
        w_branch=w_branch, w_o=w_o, moe_w1=moe_w1, moe_w3=moe_w3, moe_w2=moe_w2))
    rope_tabs = _rope_tables(ctx_len, length, k_norm_g, q_norm_g)
    rw = jnp.pad(router_w, ((0, 0), (0, LANES - N_EXPERTS)))
    rb = jnp.pad(router_b, (0, LANES - N_EXPERTS))[None, :]

    rows = -(-(b + 1) // 8) * 8
    cvec = jnp.concatenate([c, c_ctx[None, :], jnp.zeros((rows - b - 1, d), F32)], 0)
    mod = _modulation(cvec, mod_w, mod_b).reshape(DEPTH, rows, N_MOD, d)
    mod = jnp.stack([jnp.broadcast_to(mod[:, b:b + 1], (DEPTH, b, N_MOD, d)), mod[:, :b]], axis=2)

    xs = jnp.concatenate([ctx, x], axis=1)
    chunk_experts = 4
    for i in range(DEPTH):
        last = i == DEPTH - 1
        first = n_ctx_tiles if last else 0
        q, k, v = _qkv(xs, mod, lw, i, rope_tabs, n_ctx_tiles)
        attn = _attention(q, k, v, first, n_ctx_tiles)
        x_new, h2, cmb = _mix(xs, mod, attn, lw, i, first, n_ctx_tiles, rw, rb)
        cmb = cmb[..., :N_EXPERTS].reshape(b, -1, N_EXPERTS // chunk_experts, chunk_experts)
        cmb = jnp.moveaxis(cmb, 2, 0)
        xs = _moe(x_new, h2, cmb, mod, lw, i, 0 if last else ctx_len, chunk_experts)
    return xs
```

```python
import functools

import jax
import jax.numpy as jnp
from jax import lax
from jax.experimental import pallas as pl
from jax.experimental.pallas import tpu as pltpu

F32 = jnp.float32
BF16 = jnp.bfloat16

D_MODEL = 1024
DEPTH = 4
GRID_W = 64
EPS = 1e-6
N_MOD = 6
N_BRANCH = 4
BRANCH_W = 512
N_HEADS = 8
Q_LORA = 256
KV_LORA = 128
NOPE_DIM = 64
ROPE_DIM = 32
V_DIM = 64
QK_DIM = NOPE_DIM + ROPE_DIM
ROPE_HALF = ROPE_DIM // 2
AXIS_PAIRS = ROPE_DIM // 4
ROPE_THETA = 10000.0
ATTN_SCALE = QK_DIM ** -0.5
LOG2E = 1.4426950408889634
CONV_K = 3
POOL_WINDOWS = (2, 4, 8, 16)
POOL_GC = 128
SGU_CHUNK = 128
SGU_G = 4
N_EXPERTS = 16
N_EXPERT_GROUPS = 4
EXPERTS_PER_GROUP = 4
EXPERT_FF = 256

KV_OFF = 0
KR_OFF = KV_OFF + KV_LORA
Q_OFF = KR_OFF + ROPE_DIM
CONV_OFF = Q_OFF + Q_LORA
POOL_OFF = CONV_OFF + 3 * BRANCH_W
SGU_OFF = POOL_OFF + BRANCH_W
GATE_OFF = SGU_OFF + 2 * BRANCH_W

LANES = 128
HALO = 16
TOKEN_TILE = 256
SGU_TILE = 256
MIX_TILES = (512, 256)
VMEM_LIMIT = 56 * 1024 * 1024


def _rms(x, width=None):
    width = x.shape[-1] if width is None else width
    return x * lax.rsqrt(jnp.sum(x * x, axis=-1, keepdims=True) * (1.0 / width) + EPS)


def _norm_mod(x, g, shift, scale):
    return (_rms(x) * g) * (1.0 + scale) + shift


def _dot(a, b):
    return jnp.dot(a, b, preferred_element_type=F32)


def _const_spec(shape, index_map):
    return pl.BlockSpec(shape, index_map, pipeline_mode=pl.Buffered(1))


def _mod_kernel(c_ref, w_ref, b_ref, o_ref):
    c = c_ref[...]
    sc = c * jax.nn.sigmoid(c)
    o_ref[...] = jnp.dot(sc, w_ref[...], preferred_element_type=F32,
                         precision=lax.Precision.HIGHEST) + b_ref[...]


def _modulation(cvec, mod_w, mod_b):
    rows = cvec.shape[0]
    n = mod_w.shape[-1]
    bn = 1536
    return pl.pallas_call(
        _mod_kernel,
        grid=(DEPTH, n // bn),
        in_specs=[
            pl.BlockSpec((rows, D_MODEL), lambda i, j: (0, 0)),
            pl.BlockSpec((None, D_MODEL, bn), lambda i, j: (i, 0, j)),
            pl.BlockSpec((None, 1, bn), lambda i, j: (i, 0, j)),
        ],
        out_specs=pl.BlockSpec((None, rows, bn), lambda i, j: (i, 0, j)),
        out_shape=jax.ShapeDtypeStruct((DEPTH, rows, n), F32),
        compiler_params=pltpu.CompilerParams(
            dimension_semantics=("arbitrary", "arbitrary"), vmem_limit_bytes=VMEM_LIMIT),
    )(cvec, mod_w, mod_b.reshape(DEPTH, 1, n))


def _qkv_kernel(x_ref, mod_ref, g1_ref, wa_ref, gkv_ref, gq_ref, wuk_ref, wuv_ref, wuq_ref, wuqs_ref,
                gak_ref, gbk_ref, gaq_ref, gbq_ref, q_ref, k_ref, v_ref):
    x = x_ref[...]
    m = mod_ref[...]
    h = _norm_mod(x, g1_ref[...], m[0:1], m[1:2]).astype(BF16)
    p = _dot(h, wa_ref[...])
    ckv = (_rms(p[:, :KV_LORA]) * gkv_ref[...]).astype(BF16)
    kk = _dot(ckv, wuk_ref[...])
    vv = _dot(ckv, wuv_ref[...])
    kr = p[:, LANES:2 * LANES]
    krs = p[:, 2 * LANES:3 * LANES]
    cq = (_rms(p[:, 3 * LANES:]) * gq_ref[...]).astype(BF16)
    qq = _dot(cq, wuq_ref[...])
    qs = _dot(cq, wuqs_ref[...])
    gak, gbk, gaq, gbq = gak_ref[...], gbk_ref[...], gaq_ref[...], gbq_ref[...]
    k_shared = kr * gak + krs * gbk
    kr_sq = jnp.sum(kr * kr, axis=-1, keepdims=True)
    lane = lax.broadcasted_iota(jnp.int32, (1, LANES), 1)
    one_col = [jnp.where(lane == V_DIM, 1.0, 0.0), jnp.where(lane == 0, 1.0, 0.0)]

    for hd in range(N_HEADS):
        cols = slice(hd * LANES, (hd + 1) * LANES)
        kn = kk[:, cols]
        rk = lax.rsqrt((jnp.sum(kn * kn, axis=-1, keepdims=True) + kr_sq) * (1.0 / QK_DIM) + EPS)
        k_ref[hd] = ((kn * gak + k_shared) * rk).astype(BF16)
        qh = qq[:, cols]
        rq = lax.rsqrt(jnp.sum(qh * qh, axis=-1, keepdims=True) * (1.0 / QK_DIM) + EPS)
        q_ref[hd] = ((qh * gaq + qs[:, cols] * gbq) * rq).astype(BF16)
        v_ref[hd] = (vv[:, cols] + one_col[hd % 2]).astype(BF16)


def _qkv(xs, mod, lw, i, rope_tabs, lat_len):
    b, s, d = xs.shape
    tm = TOKEN_TILE
    n_lat_tiles = lat_len // tm

    def cw(shape):
        return _const_spec((None,) + shape, lambda bb, j: (i,) + (0,) * len(shape))

    head = lambda bb, j: (bb, 0, j, 0)
    rspec = pl.BlockSpec((None, tm, LANES), lambda bb, j: (i, j, 0))
    hspec = pl.BlockSpec((None, N_HEADS, tm, LANES), head)
    hshape = jax.ShapeDtypeStruct((b, N_HEADS, s, LANES), BF16)
    return pl.pallas_call(
        _qkv_kernel,
        grid=(b, s // tm),
        in_specs=[
            pl.BlockSpec((None, tm, d), lambda bb, j: (bb, j, 0)),
            pl.BlockSpec((None, None, None, N_MOD, d),
                         lambda bb, j: (i, bb, jnp.where(j < n_lat_tiles, 1, 0), 0, 0)),
            cw((1, d)), cw((d, 5 * LANES)), cw((1, KV_LORA)), cw((1, Q_LORA)),
            cw((KV_LORA, N_HEADS * LANES)), cw((KV_LORA, N_HEADS * LANES)),
            cw((Q_LORA, N_HEADS * LANES)), cw((Q_LORA, N_HEADS * LANES)),
            rspec, rspec, rspec, rspec,
        ],
        out_specs=[hspec, hspec, hspec],
        out_shape=[hshape, hshape, hshape],
        compiler_params=pltpu.CompilerParams(
            dimension_semantics=("arbitrary", "arbitrary"), vmem_limit_bytes=VMEM_LIMIT),
    )(xs, mod, lw["g1"], lw["w_a"], lw["g_kv"], lw["g_qa"], lw["w_uk"], lw["w_uv"], lw["w_uq"],
      lw["w_uqs"], *rope_tabs)


def _attn_kernel(q_ref, k_ref, v_ref, o_ref, *, lat_len, with_ctx):
    n_keys = k_ref.shape[1]
    lane = lax.broadcasted_iota(jnp.int32, (1, LANES), 1)

    def attend(lo, hi):
        for pair in range(N_HEADS // 2):
            outs = []
            for hd in (2 * pair, 2 * pair + 1):
                s = lax.dot_general(q_ref[hd], k_ref[hd, lo:hi, :], (((1,), (1,)), ((), ())),
                                    preferred_element_type=F32)
                e = jnp.exp2(s - jnp.max(s, axis=-1, keepdims=True))
                o = _dot(e.astype(BF16), v_ref[hd, lo:hi, :])
                den = o[:, V_DIM:V_DIM + 1] if hd % 2 == 0 else o[:, 0:1]
                outs.append(o / den)
            o_ref[:, pair * LANES:(pair + 1) * LANES] = jnp.where(
                lane < V_DIM, outs[0], outs[1]).astype(BF16)

    if not with_ctx:
        attend(0, n_keys)
    else:
        j = pl.program_id(1)
        n_lat_tiles = lat_len // q_ref.shape[1]
        pl.when(j < n_lat_tiles)(lambda: attend(0, n_keys))
        pl.when(j >= n_lat_tiles)(lambda: attend(lat_len, n_keys))


def _attention(q, k, v, lat_len, with_ctx):
    b, _, s, _ = q.shape
    tm = TOKEN_TILE
    n_q = (s if with_ctx else lat_len) // tm
    kern = functools.partial(_attn_kernel, lat_len=lat_len, with_ctx=with_ctx)
    return pl.pallas_call(
        kern,
        grid=(b, n_q),
        in_specs=[
            pl.BlockSpec((None, N_HEADS, tm, LANES), lambda bb, j: (bb, 0, j, 0)),
            pl.BlockSpec((None, N_HEADS, s, LANES), lambda bb, j: (bb, 0, 0, 0)),
            pl.BlockSpec((None, N_HEADS, s, LANES), lambda bb, j: (bb, 0, 0, 0)),
        ],
        out_specs=pl.BlockSpec((None, tm, BRANCH_W), lambda bb, j: (bb, j, 0)),
        out_shape=jax.ShapeDtypeStruct((b, s, BRANCH_W), BF16),
        compiler_params=pltpu.CompilerParams(
            dimension_semantics=("arbitrary", "arbitrary"), vmem_limit_bytes=VMEM_LIMIT),
    )(q, k, v)


def _route(logits, rbias):
    row = lax.broadcasted_iota(jnp.int32, logits.shape, 0)
    slot = row % EXPERTS_PER_GROUP
    s = jax.nn.sigmoid(logits)
    sb = s + rbias

    def mate(x, dd):
        fwd = pltpu.roll(x, N_EXPERTS - dd, 0)
        back = pltpu.roll(x, EXPERTS_PER_GROUP - dd, 0)
        return jnp.where(slot + dd < EXPERTS_PER_GROUP, fwd, back)

    mates = [mate(sb, dd) for dd in (1, 2, 3)]
    vals = [sb] + mates
    gscore = None
    for a in range(4):
        for c in range(a + 1, 4):
            pair = vals[a] + vals[c]
            gscore = pair if gscore is None else jnp.maximum(gscore, pair)
    group_ok = None
    for gg in (1, 2, 3):
        other = pltpu.roll(gscore, EXPERTS_PER_GROUP * gg, 0)
        other_is_lower = row >= EXPERTS_PER_GROUP * gg
        wins = jnp.logical_or(gscore > other,
                              jnp.logical_and(gscore == other, jnp.logical_not(other_is_lower)))
        group_ok = wins if group_ok is None else jnp.logical_and(group_ok, wins)
    rank = jnp.zeros(logits.shape, jnp.int32)
    for dd, mv in zip((1, 2, 3), mates):
        wrapped = slot + dd >= EXPERTS_PER_GROUP
        beats = jnp.logical_or(mv > sb, jnp.logical_and(wrapped, mv == sb))
        rank = rank + beats.astype(jnp.int32)
    chosen = jnp.logical_and(rank < 2, group_ok)
    w = jnp.where(chosen, s, 0.0)
    return w / jnp.sum(w, axis=0, keepdims=True)


N_MIX_INPUTS = 19


def _mix_kernel(*refs, seg_tiles):
    (x_ref, xp_ref, xn_ref, mod_ref, g1_ref, g2_ref, wh_ref, wm_ref, attn_ref,
     convw_ref, poolw_ref, pscale_ref, gsgu_ref, sguw_ref, sgub_ref, wbr_ref, wo_ref,
     rw_ref, rb_ref) = refs[:N_MIX_INPUTS]
    xo_ref, h2_ref, cmb_ref, hcat_ref = refs[-4:]
    tm = x_ref.shape[0]
    j = pl.program_id(1)
    m = mod_ref[...]
    g1 = g1_ref[...]
    x = x_ref[...]

    def hnorm(xv):
        return _norm_mod(xv, g1, m[0:1], m[1:2]).astype(BF16)

    hm = hnorm(x)
    hcat_ref[0:HALO, :] = hnorm(xp_ref[...])
    hcat_ref[HALO:HALO + tm, :] = hm
    hcat_ref[HALO + tm:, :] = hnorm(xn_ref[...])

    nrow = tm + 2 * HALO
    row = lax.broadcasted_iota(jnp.int32, (nrow, 1), 0)
    keep = jnp.logical_and(jnp.logical_or(row >= HALO, j > 0),
                           jnp.logical_or(row < HALO + tm, j < seg_tiles - 1))
    ph = _dot(hcat_ref[...], wh_ref[...])
    ph = jnp.where(keep, ph, 0.0)

    def shifted(a, k):
        return pltpu.roll(a, (-k) % nrow, 0)[HALO:HALO + tm]

    cu = ph[:, :BRANCH_W] * ph[:, BRANCH_W:2 * BRANCH_W]
    cw = convw_ref[...]
    conv = shifted(cu, -1) * cw[0:1] + cu[HALO:HALO + tm] * cw[1:2] + shifted(cu, 1) * cw[2:3]
    gb = _dot(hm, wm_ref[:, 0:BRANCH_W])
    br_conv = gb * conv

    pp = ph[:, 2 * BRANCH_W:]
    seg_len = seg_tiles * tm
    pos = j * tm + lax.broadcasted_iota(jnp.int32, (tm, 1), 0)
    level = pp + pltpu.roll(pp, 1, 0)
    pooled = []
    for gi, win in enumerate(POOL_WINDOWS):
        if gi > 0:
            half = win // 4
            level = level[:, POOL_GC:]
            level = pltpu.roll(level, nrow - half, 0) + pltpu.roll(level, half, 0)
        cnt = (jnp.minimum(pos + win // 2, seg_len) - jnp.maximum(pos - win // 2, 0)).astype(F32)
        own = pp[HALO:HALO + tm, gi * POOL_GC:(gi + 1) * POOL_GC]
        pooled.append(level[HALO:HALO + tm, :POOL_GC] / cnt - own)
    br_pool = jnp.concatenate(
        [_dot(pooled[gi].astype(BF16), poolw_ref[gi]) for gi in range(len(POOL_WINDOWS))], axis=-1)
    br_pool = br_pool * pscale_ref[...]

    z = jax.nn.gelu(_dot(hm, wm_ref[:, BRANCH_W:3 * BRANCH_W]), approximate=True)
    zu = z[:, :BRANCH_W]
    zv = (_rms(z[:, BRANCH_W:]) * gsgu_ref[...]).astype(BF16)
    mixed = jnp.concatenate([
        jnp.concatenate([_dot(sguw_ref[gi], zv[r0:r0 + SGU_TILE, gi * LANES:(gi + 1) * LANES])
                         for gi in range(SGU_G)], axis=-1) + sgub_ref[...]
        for r0 in range(0, tm, SGU_TILE)], axis=0)
    br_sgu = zu * mixed

    branches = (attn_ref[...], br_conv.astype(BF16), br_pool.astype(BF16), br_sgu.astype(BF16))
    merged = None
    for nb, br in enumerate(branches):
        lo = 3 * BRANCH_W + nb * D_MODEL
        gate = jax.nn.sigmoid(_dot(hm, wm_ref[:, lo:lo + D_MODEL]))
        term = gate * _dot(br, wbr_ref[nb])
        merged = term if merged is None else merged + term
    x_new = x + m[2:3] * _dot(merged.astype(BF16), wo_ref[...])
    xo_ref[...] = x_new

    h2 = _norm_mod(x_new, g2_ref[...], m[3:4], m[4:5])
    h2_ref[...] = h2.astype(BF16)
    logits = lax.dot_general(rw_ref[...], h2, (((1,), (1,)), ((), ())),
                             preferred_element_type=F32, precision=lax.Precision.HIGHEST)
    cmb_t = _route(logits, rb_ref[...])
    pad = jnp.zeros((LANES - N_EXPERTS, tm), F32)
    cmb_ref[...] = jnp.concatenate([cmb_t, pad], axis=0).T


def _mix(xs, mod, attn, lw, i, seg_start, seg_len, seg_id, out_len, carry, router_w, router_b):
    b, s, d = xs.shape
    tm = next(t for t in MIX_TILES if seg_len % t == 0 and seg_start % t == 0)
    seg_tiles = seg_len // tm
    base = seg_start // tm
    hb = tm // HALO
    base_h = seg_start // HALO
    last_h = s // HALO - 1

    def cw(shape):
        return _const_spec((None,) + shape, lambda bb, j: (i,) + (0,) * len(shape))

    tok = lambda bb, j: (bb, base + j, 0)
    in_specs = [
        pl.BlockSpec((None, tm, d), tok),
        pl.BlockSpec((None, HALO, d), lambda bb, j: (bb, jnp.maximum(base_h + j * hb - 1, 0), 0)),
        pl.BlockSpec((None, HALO, d),
                     lambda bb, j: (bb, jnp.minimum(base_h + (j + 1) * hb, last_h), 0)),
        pl.BlockSpec((None, None, None, N_MOD, d), lambda bb, j: (i, bb, seg_id, 0, 0)),
        cw((1, d)), cw((1, d)),
        cw((d, 3 * BRANCH_W)), cw((d, 3 * BRANCH_W + N_BRANCH * D_MODEL)),
        pl.BlockSpec((None, tm, BRANCH_W), tok),
        cw((CONV_K, BRANCH_W)), cw((len(POOL_WINDOWS), POOL_GC, POOL_GC)), cw((1, BRANCH_W)),
        cw((1, BRANCH_W)), cw((SGU_G, SGU_TILE, SGU_TILE)), cw((SGU_TILE, BRANCH_W)),
        cw((N_BRANCH, BRANCH_W, D_MODEL)), cw((D_MODEL, D_MODEL)),
        _const_spec((N_EXPERTS, d), lambda bb, j: (0, 0)),
        _const_spec((N_EXPERTS, 1), lambda bb, j: (0, 0)),
    ]
    assert len(in_specs) == N_MIX_INPUTS
    args = [xs, xs, xs, mod, lw["g1"], lw["g2"], lw["w_h"], lw["w_m"], attn,
            lw["conv_w"], lw["pool_w"], lw["pool_scale"], lw["g_sgu"], lw["sgu_w"], lw["sgu_b"],
            lw["w_branch"], lw["w_o"], router_w, router_b]
    aliases = {}
    if carry is not None:
        in_specs += [pl.BlockSpec(memory_space=pl.ANY)] * 3
        args += list(carry)
        aliases = {N_MIX_INPUTS + n: n for n in range(3)}
    return pl.pallas_call(
        functools.partial(_mix_kernel, seg_tiles=seg_tiles),
        grid=(b, seg_tiles),
        in_specs=in_specs,
        out_specs=[
            pl.BlockSpec((None, tm, d), tok),
            pl.BlockSpec((None, tm, d), tok),
            pl.BlockSpec((None, tm, LANES), tok),
        ],
        out_shape=[
            jax.ShapeDtypeStruct((b, out_len, d), F32),
            jax.ShapeDtypeStruct((b, out_len, d), BF16),
            jax.ShapeDtypeStruct((b, out_len, LANES), F32),
        ],
        input_output_aliases=aliases,
        scratch_shapes=[pltpu.VMEM((tm + 2 * HALO, d), BF16)],
        compiler_params=pltpu.CompilerParams(
            dimension_semantics=("arbitrary", "arbitrary"), vmem_limit_bytes=VMEM_LIMIT),
    )(*args)


def _moe_kernel(x_ref, h_ref, cmb_ref, mod_ref, w1_ref, w3_ref, w2_ref, o_ref, *, row_chunk, lat_rows):
    tile = x_ref.shape[0]
    ec = pl.program_id(2)
    n_exp = w1_ref.shape[0]

    @pl.when(ec == 0)
    def _():
        o_ref[...] = x_ref[...]

    for r in range(tile // row_chunk):
        rows = pl.ds(r * row_chunk, row_chunk)
        gate = mod_ref[1, N_MOD - 1:N_MOD]
        if lat_rows is not None:
            pos = (pl.program_id(1) * tile + r * row_chunk
                   + lax.broadcasted_iota(jnp.int32, (row_chunk, 1), 0))
            gate = jnp.where(pos < lat_rows, gate, mod_ref[0, N_MOD - 1:N_MOD])
        h = h_ref[rows, :]
        cmb = cmb_ref[rows, :]
        acts = []
        for e in range(n_exp):
            a = _dot(h, w1_ref[e])
            a = (a * jax.nn.sigmoid(a)) * _dot(h, w3_ref[e])
            acts.append((a * cmb[:, e:e + 1]).astype(BF16))
        o_ref[rows, :] += gate * _dot(jnp.concatenate(acts, axis=-1), w2_ref[...])


def _moe_tile(s):
    for tile, chunk in ((1152, 384), (1024, 512), (768, 384), (512, 256), (256, 256)):
        if s % tile == 0:
            return tile, chunk
    raise ValueError(f"unsupported sequence length {s}")


def _moe(x_new, h2, cmb, mod, lw, i, lat_rows, chunk_experts):
    b, s, d = x_new.shape
    tile, row_chunk = _moe_tile(s)
    n_chunks = N_EXPERTS // chunk_experts
    cf = chunk_experts * EXPERT_FF
    tok = lambda bb, j, e: (bb, j, 0)
    kern = functools.partial(_moe_kernel, row_chunk=row_chunk, lat_rows=lat_rows)
    return pl.pallas_call(
        kern,
        grid=(b, s // tile, n_chunks),
        in_specs=[
            pl.BlockSpec((None, tile, d), tok),
            pl.BlockSpec((None, tile, d), tok),
            pl.BlockSpec((None, None, tile, chunk_experts), lambda bb, j, e: (e, bb, j, 0)),
            pl.BlockSpec((None, None, 2, N_MOD, d), lambda bb, j, e: (i, bb, 0, 0, 0)),
            pl.BlockSpec((None, chunk_experts, d, EXPERT_FF), lambda bb, j, e: (i, e, 0, 0)),
            pl.BlockSpec((None, chunk_experts, d, EXPERT_FF), lambda bb, j, e: (i, e, 0, 0)),
            pl.BlockSpec((None, cf, d), lambda bb, j, e: (i, e, 0)),
        ],
        out_specs=pl.BlockSpec((None, tile, d), tok),
        out_shape=jax.ShapeDtypeStruct((b, s, d), F32),
        input_output_aliases={0: 0},
        compiler_params=pltpu.CompilerParams(
            dimension_semantics=("arbitrary", "arbitrary", "arbitrary"), vmem_limit_bytes=VMEM_LIMIT),
    )(x_new, h2, cmb, mod, lw["w1"], lw["w3"], lw["w2"])


def _prepare(p):
    dpt = p["w_in"].shape[0]
    w_in = p["w_in"]
    zeros = lambda n: jnp.zeros((dpt, D_MODEL, n), F32)
    kr_mid = KR_OFF + ROPE_HALF
    w_a = jnp.concatenate([
        w_in[..., KV_OFF:KR_OFF],
        zeros(NOPE_DIM), w_in[..., KR_OFF:Q_OFF], zeros(LANES - QK_DIM),
        zeros(NOPE_DIM), w_in[..., kr_mid:Q_OFF], w_in[..., KR_OFF:kr_mid], zeros(LANES - QK_DIM),
        w_in[..., Q_OFF:CONV_OFF]], axis=-1)
    gb_lo, gc_lo, u_lo = CONV_OFF, CONV_OFF + BRANCH_W, CONV_OFF + 2 * BRANCH_W
    w_h = jnp.concatenate([w_in[..., gc_lo:u_lo], w_in[..., u_lo:POOL_OFF],
                           w_in[..., POOL_OFF:SGU_OFF]], axis=-1)
    w_m = jnp.concatenate([w_in[..., gb_lo:gc_lo], w_in[..., SGU_OFF:]], axis=-1)
    w_ukv = p["w_ukv"]
    w_uk = jnp.concatenate([w_ukv[..., :NOPE_DIM], jnp.zeros_like(w_ukv[..., :NOPE_DIM])], axis=-1)
    w_v = w_ukv[..., NOPE_DIM:]
    even = (jnp.arange(N_HEADS) % 2 == 0)[None, None, :, None]
    w_uv = jnp.concatenate([jnp.where(even, w_v, 0.0), jnp.where(even, 0.0, w_v)], axis=-1)
    w_q = p["w_uq"]
    w_uq = jnp.pad(w_q, ((0, 0), (0, 0), (0, 0), (0, LANES - QK_DIM)))
    rope_mid = NOPE_DIM + ROPE_HALF
    w_uqs = jnp.concatenate([
        jnp.zeros_like(w_q[..., :NOPE_DIM]), w_q[..., rope_mid:], w_q[..., NOPE_DIM:rope_mid],
        jnp.zeros_like(w_q[..., :LANES - QK_DIM])], axis=-1)
    eye = jnp.eye(SGU_TILE // SGU_CHUNK, dtype=F32)
    sgu_bd = jnp.einsum("ab,lgpq->lgapbq", eye, p["sgu_w"]).reshape(dpt, SGU_G, SGU_TILE, SGU_TILE)
    sgu_b = jnp.repeat(jnp.swapaxes(p["sgu_b"], 1, 2), SGU_CHUNK, axis=2)
    sgu_b = jnp.tile(sgu_b, (1, SGU_TILE // SGU_CHUNK, 1))
    return {
        "g1": p["norm1_g"][:, None, :], "g2": p["norm2_g"][:, None, :],
        "w_a": w_a.astype(BF16), "w_h": w_h.astype(BF16), "w_m": w_m.astype(BF16),
        "g_kv": p["kv_a_norm_g"][:, None, :], "g_qa": p["q_a_norm_g"][:, None, :],
        "w_uk": w_uk.reshape(dpt, KV_LORA, N_HEADS * LANES).astype(BF16),
        "w_uv": w_uv.reshape(dpt, KV_LORA, N_HEADS * LANES).astype(BF16),
        "w_uq": w_uq.reshape(dpt, Q_LORA, N_HEADS * LANES).astype(BF16),
        "w_uqs": w_uqs.reshape(dpt, Q_LORA, N_HEADS * LANES).astype(BF16),
        "conv_w": p["conv_w"], "pool_w": p["pool_w"].astype(BF16),
        "pool_scale": p["pool_scale"][:, None, :], "g_sgu": p["sgu_norm_g"][:, None, :],
        "sgu_w": sgu_bd.astype(BF16), "sgu_b": sgu_b,
        "w_branch": p["w_branch"].astype(BF16), "w_o": p["w_o"].astype(BF16),
        "w1": p["moe_w1"].astype(BF16), "w3": p["moe_w3"].astype(BF16),
        "w2": p["moe_w2"].reshape(dpt, N_EXPERTS * EXPERT_FF, D_MODEL).astype(BF16),
    }


def _rope_tables(ctx_len, length, k_gain, q_gain):
    rows = length // GRID_W
    t = jnp.arange(rows * GRID_W)
    grid = jnp.stack([t // GRID_W, t % GRID_W], -1).astype(F32)
    inv_freq = ROPE_THETA ** (-jnp.arange(AXIS_PAIRS, dtype=F32) / AXIS_PAIRS)
    ang = jnp.concatenate([grid[:, :1] * inv_freq, grid[:, 1:] * inv_freq], -1)
    cos, sin = jnp.cos(ang), jnp.sin(ang)
    one = jnp.ones((length, NOPE_DIM), F32)
    tail = jnp.ones((length, LANES - QK_DIM), F32)
    zn = jnp.zeros((length, NOPE_DIM), F32)
    zt = jnp.zeros((length, LANES - QK_DIM), F32)
    rc = jnp.concatenate([one, cos, cos, tail], -1)
    rs = jnp.concatenate([zn, -sin, sin, zt], -1)
    ctx_id = lambda fill: jnp.full((ctx_len, LANES), fill, F32)
    rc = jnp.concatenate([rc, ctx_id(1.0)], 0)
    rs = jnp.concatenate([rs, ctx_id(0.0)], 0)

    def tables(gain, const):
        g = jnp.pad(gain, ((0, 0), (0, LANES - QK_DIM)))
        mid = NOPE_DIM + ROPE_HALF
        g_swap = jnp.concatenate([g[:, :NOPE_DIM], g[:, mid:QK_DIM], g[:, NOPE_DIM:mid],
                                  g[:, QK_DIM:]], -1)
        return (g[:, None, :] * rc[None]) * const, (g_swap[:, None, :] * rs[None]) * const

    gak, gbk = tables(k_gain, 1.0)
    gaq, gbq = tables(q_gain, ATTN_SCALE * LOG2E)
    return gak, gbk, gaq, gbq


def kernel(x, c, ctx, c_ctx, mod_w, mod_b, norm1_g, norm2_g, w_in, kv_a_norm_g, w_ukv, k_norm_g,
           q_a_norm_g, w_uq, q_norm_g, conv_w, pool_w, pool_scale, sgu_norm_g, sgu_w, sgu_b,
           w_branch, w_o, router_w, router_b, moe_w1, moe_w3, moe_w2):
    b, length, d = x.shape
    ctx_len = ctx.shape[1]
    assert d == D_MODEL and length % TOKEN_TILE == 0 and ctx_len % TOKEN_TILE == 0
    assert length % GRID_W == 0
    s = length + ctx_len

    lw = _prepare(dict(
        norm1_g=norm1_g, norm2_g=norm2_g, w_in=w_in, kv_a_norm_g=kv_a_norm_g, w_ukv=w_ukv,
        k_norm_g=k_norm_g, q_a_norm_g=q_a_norm_g, w_uq=w_uq, q_norm_g=q_norm_g, conv_w=conv_w,
        pool_w=pool_w, pool_scale=pool_scale, sgu_norm_g=sgu_norm_g, sgu_w=sgu_w, sgu_b=sgu_b,
        w_branch=w_branch, w_o=w_o, moe_w1=moe_w1, moe_w3=moe_w3, moe_w2=moe_w2))
    rope_tabs = _rope_tables(ctx_len, length, k_norm_g, q_norm_g)
    rw = router_w.T
    rb = router_b[:, None]

    rows = -(-(b + 1) // 8) * 8
    cvec = jnp.concatenate([c, c_ctx[None, :], jnp.zeros((rows - b - 1, d), F32)], 0)
    mod = _modulation(cvec, mod_w, mod_b).reshape(DEPTH, rows, N_MOD, d)
    mod = jnp.stack([jnp.broadcast_to(mod[:, b:b + 1], (DEPTH, b, N_MOD, d)), mod[:, :b]], axis=2)

    xs = jnp.concatenate([x, ctx], axis=1)
    chunk_experts = 4
    for i in range(DEPTH):
        last = i == DEPTH - 1
        q, k, v = _qkv(xs, mod, lw, i, rope_tabs, length)
        attn = _attention(q, k, v, length, not last)
        out_len = length if last else s
        outs = _mix(xs, mod, attn, lw, i, 0, length, 1, out_len, None, rw, rb)
        if not last:
            outs = _mix(xs, mod, attn, lw, i, length, ctx_len, 0, out_len, outs, rw, rb)
        x_new, h2, cmb = outs
        cmb = cmb[..., :N_EXPERTS].reshape(b, -1, N_EXPERTS // chunk_experts, chunk_experts)
        cmb = jnp.moveaxis(cmb, 2, 0)
        xs = _moe(x_new, h2, cmb, mod, lw, i, None if last else length, chunk_experts)
    return xs
```

```python
import functools

import jax
import jax.numpy as jnp
from jax import lax
from jax.experimental import pallas as pl
from jax.experimental.pallas import tpu as pltpu

F32 = jnp.float32
BF16 = jnp.bfloat16

D_MODEL = 1024
DEPTH = 4
GRID_W = 64
EPS = 1e-6
N_MOD = 6
N_BRANCH = 4
BRANCH_W = 512
N_HEADS = 8
Q_LORA = 256
KV_LORA = 128
NOPE_DIM = 64
ROPE_DIM = 32
V_DIM = 64
QK_DIM = NOPE_DIM + ROPE_DIM
ROPE_HALF = ROPE_DIM // 2
AXIS_PAIRS = ROPE_DIM // 4
ROPE_THETA = 10000.0
ATTN_SCALE = QK_DIM ** -0.5
LOG2E = 1.4426950408889634
CONV_K = 3
POOL_WINDOWS = (2, 4, 8, 16)
POOL_GC = 128
SGU_CHUNK = 128
SGU_G = 4
N_EXPERTS = 16
N_EXPERT_GROUPS = 4
EXPERTS_PER_GROUP = 4
EXPERT_FF = 256

KV_OFF = 0
KR_OFF = KV_OFF + KV_LORA
Q_OFF = KR_OFF + ROPE_DIM
CONV_OFF = Q_OFF + Q_LORA
POOL_OFF = CONV_OFF + 3 * BRANCH_W
SGU_OFF = POOL_OFF + BRANCH_W
GATE_OFF = SGU_OFF + 2 * BRANCH_W

LANES = 128
HALO = 16
TOKEN_TILE = 256
SGU_TILE = 256
MIX_TILES = (512, 256)
VMEM_LIMIT = 56 * 1024 * 1024


def _rms(x, width=None):
    width = x.shape[-1] if width is None else width
    return x * lax.rsqrt(jnp.sum(x * x, axis=-1, keepdims=True) * (1.0 / width) + EPS)


def _norm_mod(x, g, shift, scale):
    return (_rms(x) * g) * (1.0 + scale) + shift


def _dot(a, b):
    return jnp.dot(a, b, preferred_element_type=F32)


def _const_spec(shape, index_map):
    return pl.BlockSpec(shape, index_map, pipeline_mode=pl.Buffered(1))


def _mod_kernel(c_ref, w_ref, b_ref, o_ref):
    c = c_ref[...]
    sc = c * jax.nn.sigmoid(c)
    o_ref[...] = jnp.dot(sc, w_ref[...], preferred_element_type=F32,
                         precision=lax.Precision.HIGHEST) + b_ref[...]


def _modulation(cvec, mod_w, mod_b):
    rows = cvec.shape[0]
    n = mod_w.shape[-1]
    bn = 1536
    return pl.pallas_call(
        _mod_kernel,
        grid=(DEPTH, n // bn),
        in_specs=[
            pl.BlockSpec((rows, D_MODEL), lambda i, j: (0, 0)),
            pl.BlockSpec((None, D_MODEL, bn), lambda i, j: (i, 0, j)),
            pl.BlockSpec((None, 1, bn), lambda i, j: (i, 0, j)),
        ],
        out_specs=pl.BlockSpec((None, rows, bn), lambda i, j: (i, 0, j)),
        out_shape=jax.ShapeDtypeStruct((DEPTH, rows, n), F32),
        compiler_params=pltpu.CompilerParams(
            dimension_semantics=("arbitrary", "arbitrary"), vmem_limit_bytes=VMEM_LIMIT),
    )(cvec, mod_w, mod_b.reshape(DEPTH, 1, n))


def _qkv_kernel(x_ref, mod_ref, g1_ref, wa_ref, gkv_ref, gq_ref, wuk_ref, wuv_ref, wuq_ref, wuqs_ref,
                gak_ref, gbk_ref, gaq_ref, gbq_ref, q_ref, k_ref, v_ref):
    x = x_ref[...]
    m = mod_ref[...]
    h = _norm_mod(x, g1_ref[...], m[0:1], m[1:2]).astype(BF16)
    p = _dot(h, wa_ref[...])
    ckv = (_rms(p[:, :KV_LORA]) * gkv_ref[...]).astype(BF16)
    kk = _dot(ckv, wuk_ref[...])
    vv = _dot(ckv, wuv_ref[...])
    kr = p[:, LANES:2 * LANES]
    krs = p[:, 2 * LANES:3 * LANES]
    cq = (_rms(p[:, 3 * LANES:]) * gq_ref[...]).astype(BF16)
    qq = _dot(cq, wuq_ref[...])
    qs = _dot(cq, wuqs_ref[...])
    gak, gbk, gaq, gbq = gak_ref[...], gbk_ref[...], gaq_ref[...], gbq_ref[...]
    k_shared = kr * gak + krs * gbk
    kr_sq = jnp.sum(kr * kr, axis=-1, keepdims=True)
    lane = lax.broadcasted_iota(jnp.int32, (1, LANES), 1)
    one_col = [jnp.where(lane == V_DIM, 1.0, 0.0), jnp.where(lane == 0, 1.0, 0.0)]

    for hd in range(N_HEADS):
        cols = slice(hd * LANES, (hd + 1) * LANES)
        kn = kk[:, cols]
        rk = lax.rsqrt((jnp.sum(kn * kn, axis=-1, keepdims=True) + kr_sq) * (1.0 / QK_DIM) + EPS)
        k_ref[hd] = ((kn * gak + k_shared) * rk).astype(BF16)
        qh = qq[:, cols]
        rq = lax.rsqrt(jnp.sum(qh * qh, axis=-1, keepdims=True) * (1.0 / QK_DIM) + EPS)
        q_ref[hd] = ((qh * gaq + qs[:, cols] * gbq) * rq).astype(BF16)
        v_ref[hd] = (vv[:, cols] + one_col[hd % 2]).astype(BF16)


def _qkv(xs, mod, lw, i, rope_tabs, lat_len):
    b, s, d = xs.shape
    tm = TOKEN_TILE
    n_lat_tiles = lat_len // tm

    def cw(shape):
        return _const_spec((None,) + shape, lambda bb, j: (i,) + (0,) * len(shape))

    head = lambda bb, j: (bb, 0, j, 0)
    rspec = pl.BlockSpec((None, tm, LANES), lambda bb, j: (i, j, 0))
    hspec = pl.BlockSpec((None, N_HEADS, tm, LANES), head)
    hshape = jax.ShapeDtypeStruct((b, N_HEADS, s, LANES), BF16)
    return pl.pallas_call(
        _qkv_kernel,
        grid=(b, s // tm),
        in_specs=[
            pl.BlockSpec((None, tm, d), lambda bb, j: (bb, j, 0)),
            pl.BlockSpec((None, None, None, N_MOD, d),
                         lambda bb, j: (i, bb, jnp.where(j < n_lat_tiles, 1, 0), 0, 0)),
            cw((1, d)), cw((d, 5 * LANES)), cw((1, KV_LORA)), cw((1, Q_LORA)),
            cw((KV_LORA, N_HEADS * LANES)), cw((KV_LORA, N_HEADS * LANES)),
            cw((Q_LORA, N_HEADS * LANES)), cw((Q_LORA, N_HEADS * LANES)),
            rspec, rspec, rspec, rspec,
        ],
        out_specs=[hspec, hspec, hspec],
        out_shape=[hshape, hshape, hshape],
        compiler_params=pltpu.CompilerParams(
            dimension_semantics=("arbitrary", "arbitrary"), vmem_limit_bytes=VMEM_LIMIT),
    )(xs, mod, lw["g1"], lw["w_a"], lw["g_kv"], lw["g_qa"], lw["w_uk"], lw["w_uv"], lw["w_uq"],
      lw["w_uqs"], *rope_tabs)


def _attn_kernel(q_ref, k_ref, v_ref, o_ref, *, lat_len, with_ctx):
    n_keys = k_ref.shape[1]
    lane = lax.broadcasted_iota(jnp.int32, (1, LANES), 1)

    def attend(lo, hi):
        for pair in range(N_HEADS // 2):
            outs = []
            for hd in (2 * pair, 2 * pair + 1):
                s = lax.dot_general(q_ref[hd], k_ref[hd, lo:hi, :], (((1,), (1,)), ((), ())),
                                    preferred_element_type=F32)
                e = jnp.exp2(s - jnp.max(s, axis=-1, keepdims=True))
                o = _dot(e.astype(BF16), v_ref[hd, lo:hi, :])
                den = o[:, V_DIM:V_DIM + 1] if hd % 2 == 0 else o[:, 0:1]
                outs.append(o / den)
            o_ref[:, pair * LANES:(pair + 1) * LANES] = jnp.where(
                lane < V_DIM, outs[0], outs[1]).astype(BF16)

    if not with_ctx:
        attend(0, n_keys)
    else:
        j = pl.program_id(1)
        n_lat_tiles = lat_len // q_ref.shape[1]
        pl.when(j < n_lat_tiles)(lambda: attend(0, n_keys))
        pl.when(j >= n_lat_tiles)(lambda: attend(lat_len, n_keys))


def _attention(q, k, v, lat_len, with_ctx):
    b, _, s, _ = q.shape
    tm = TOKEN_TILE
    n_q = (s if with_ctx else lat_len) // tm
    kern = functools.partial(_attn_kernel, lat_len=lat_len, with_ctx=with_ctx)
    return pl.pallas_call(
        kern,
        grid=(b, n_q),
        in_specs=[
            pl.BlockSpec((None, N_HEADS, tm, LANES), lambda bb, j: (bb, 0, j, 0)),
            pl.BlockSpec((None, N_HEADS, s, LANES), lambda bb, j: (bb, 0, 0, 0)),
            pl.BlockSpec((None, N_HEADS, s, LANES), lambda bb, j: (bb, 0, 0, 0)),
        ],
        out_specs=pl.BlockSpec((None, tm, BRANCH_W), lambda bb, j: (bb, j, 0)),
        out_shape=jax.ShapeDtypeStruct((b, s, BRANCH_W), BF16),
        compiler_params=pltpu.CompilerParams(
            dimension_semantics=("arbitrary", "arbitrary"), vmem_limit_bytes=VMEM_LIMIT),
    )(q, k, v)


def _route(logits, rbias):
    row = lax.broadcasted_iota(jnp.int32, logits.shape, 0)
    slot = row % EXPERTS_PER_GROUP
    s = jax.nn.sigmoid(logits)
    sb = s + rbias

    def mate(x, dd):
        fwd = pltpu.roll(x, N_EXPERTS - dd, 0)
        back = pltpu.roll(x, EXPERTS_PER_GROUP - dd, 0)
        return jnp.where(slot + dd < EXPERTS_PER_GROUP, fwd, back)

    mates = [mate(sb, dd) for dd in (1, 2, 3)]
    vals = [sb] + mates
    gscore = None
    for a in range(4):
        for c in range(a + 1, 4):
            pair = vals[a] + vals[c]
            gscore = pair if gscore is None else jnp.maximum(gscore, pair)
    group_ok = None
    for gg in (1, 2, 3):
        other = pltpu.roll(gscore, EXPERTS_PER_GROUP * gg, 0)
        other_is_lower = row >= EXPERTS_PER_GROUP * gg
        wins = jnp.logical_or(gscore > other,
                              jnp.logical_and(gscore == other, jnp.logical_not(other_is_lower)))
        group_ok = wins if group_ok is None else jnp.logical_and(group_ok, wins)
    rank = jnp.zeros(logits.shape, jnp.int32)
    for dd, mv in zip((1, 2, 3), mates):
        wrapped = slot + dd >= EXPERTS_PER_GROUP
        beats = jnp.logical_or(mv > sb, jnp.logical_and(wrapped, mv == sb))
        rank = rank + beats.astype(jnp.int32)
    chosen = jnp.logical_and(rank < 2, group_ok)
    w = jnp.where(chosen, s, 0.0)
    group_f = jnp.where(group_ok, 1.0, 0.0)
    group_onehot = jnp.concatenate(
        [group_f[gg * EXPERTS_PER_GROUP:gg * EXPERTS_PER_GROUP + 1]
         for gg in range(N_EXPERT_GROUPS)], axis=0)
    return w / jnp.sum(w, axis=0, keepdims=True), group_onehot


N_MIX_INPUTS = 19


def _mix_kernel(*refs, seg_tiles):
    (x_ref, xp_ref, xn_ref, mod_ref, g1_ref, g2_ref, wh_ref, wm_ref, attn_ref,
     convw_ref, poolw_ref, pscale_ref, gsgu_ref, sguw_ref, sgub_ref, wbr_ref, wo_ref,
     rw_ref, rb_ref) = refs[:N_MIX_INPUTS]
    xo_ref, h2_ref, cmb_ref, grp_ref, hcat_ref = refs[-5:]
    tm = x_ref.shape[0]
    j = pl.program_id(1)
    m = mod_ref[...]
    g1 = g1_ref[...]
    x = x_ref[...]

    def hnorm(xv):
        return _norm_mod(xv, g1, m[0:1], m[1:2]).astype(BF16)

    hm = hnorm(x)
    hcat_ref[0:HALO, :] = hnorm(xp_ref[...])
    hcat_ref[HALO:HALO + tm, :] = hm
    hcat_ref[HALO + tm:, :] = hnorm(xn_ref[...])

    nrow = tm + 2 * HALO
    row = lax.broadcasted_iota(jnp.int32, (nrow, 1), 0)
    keep = jnp.logical_and(jnp.logical_or(row >= HALO, j > 0),
                           jnp.logical_or(row < HALO + tm, j < seg_tiles - 1))
    ph = _dot(hcat_ref[...], wh_ref[...])
    ph = jnp.where(keep, ph, 0.0)

    def shifted(a, k):
        return pltpu.roll(a, (-k) % nrow, 0)[HALO:HALO + tm]

    cu = ph[:, :BRANCH_W] * ph[:, BRANCH_W:2 * BRANCH_W]
    cw = convw_ref[...]
    conv = shifted(cu, -1) * cw[0:1] + cu[HALO:HALO + tm] * cw[1:2] + shifted(cu, 1) * cw[2:3]
    gb = _dot(hm, wm_ref[:, 0:BRANCH_W])
    br_conv = gb * conv

    pp = ph[:, 2 * BRANCH_W:]
    seg_len = seg_tiles * tm
    pos = j * tm + lax.broadcasted_iota(jnp.int32, (tm, 1), 0)
    level = pp + pltpu.roll(pp, 1, 0)
    pooled = []
    for gi, win in enumerate(POOL_WINDOWS):
        if gi > 0:
            half = win // 4
            level = level[:, POOL_GC:]
            level = pltpu.roll(level, nrow - half, 0) + pltpu.roll(level, half, 0)
        cnt = (jnp.minimum(pos + win // 2, seg_len) - jnp.maximum(pos - win // 2, 0)).astype(F32)
        own = pp[HALO:HALO + tm, gi * POOL_GC:(gi + 1) * POOL_GC]
        pooled.append(level[HALO:HALO + tm, :POOL_GC] / cnt - own)
    br_pool = jnp.concatenate(
        [_dot(pooled[gi].astype(BF16), poolw_ref[gi]) for gi in range(len(POOL_WINDOWS))], axis=-1)
    br_pool = br_pool * pscale_ref[...]

    z = jax.nn.gelu(_dot(hm, wm_ref[:, BRANCH_W:3 * BRANCH_W]), approximate=True)
    zu = z[:, :BRANCH_W]
    zv = (_rms(z[:, BRANCH_W:]) * gsgu_ref[...]).astype(BF16)
    mixed = jnp.concatenate([
        jnp.concatenate([_dot(sguw_ref[gi], zv[r0:r0 + SGU_TILE, gi * LANES:(gi + 1) * LANES])
                         for gi in range(SGU_G)], axis=-1) + sgub_ref[...]
        for r0 in range(0, tm, SGU_TILE)], axis=0)
    br_sgu = zu * mixed

    branches = (attn_ref[...], br_conv.astype(BF16), br_pool.astype(BF16), br_sgu.astype(BF16))
    merged = None
    for nb, br in enumerate(branches):
        lo = 3 * BRANCH_W + nb * D_MODEL
        gate = jax.nn.sigmoid(_dot(hm, wm_ref[:, lo:lo + D_MODEL]))
        term = gate * _dot(br, wbr_ref[nb])
        merged = term if merged is None else merged + term
    x_new = x + m[2:3] * _dot(merged.astype(BF16), wo_ref[...])
    xo_ref[...] = x_new

    h2 = _norm_mod(x_new, g2_ref[...], m[3:4], m[4:5])
    h2_ref[...] = h2.astype(BF16)
    logits = lax.dot_general(rw_ref[...], h2, (((1,), (1,)), ((), ())),
                             preferred_element_type=F32, precision=lax.Precision.HIGHEST)
    cmb_t, grp_t = _route(logits, rb_ref[...])
    pad = jnp.zeros((LANES - N_EXPERTS - N_EXPERT_GROUPS, tm), F32)
    cmb_ref[...] = jnp.concatenate([cmb_t, grp_t, pad], axis=0).T
    grp_ref[...] = jnp.concatenate([grp_t, jnp.zeros_like(grp_t)], axis=0)


def _mix(xs, mod, attn, lw, i, seg_start, seg_len, seg_id, out_len, carry, router_w, router_b):
    b, s, d = xs.shape
    tm = next(t for t in MIX_TILES if seg_len % t == 0 and seg_start % t == 0)
    seg_tiles = seg_len // tm
    base = seg_start // tm
    hb = tm // HALO
    base_h = seg_start // HALO
    last_h = s // HALO - 1

    def cw(shape):
        return _const_spec((None,) + shape, lambda bb, j: (i,) + (0,) * len(shape))

    tok = lambda bb, j: (bb, base + j, 0)
    in_specs = [
        pl.BlockSpec((None, tm, d), tok),
        pl.BlockSpec((None, HALO, d), lambda bb, j: (bb, jnp.maximum(base_h + j * hb - 1, 0), 0)),
        pl.BlockSpec((None, HALO, d),
                     lambda bb, j: (bb, jnp.minimum(base_h + (j + 1) * hb, last_h), 0)),
        pl.BlockSpec((None, None, None, N_MOD, d), lambda bb, j: (i, bb, seg_id, 0, 0)),
        cw((1, d)), cw((1, d)),
        cw((d, 3 * BRANCH_W)), cw((d, 3 * BRANCH_W + N_BRANCH * D_MODEL)),
        pl.BlockSpec((None, tm, BRANCH_W), tok),
        cw((CONV_K, BRANCH_W)), cw((len(POOL_WINDOWS), POOL_GC, POOL_GC)), cw((1, BRANCH_W)),
        cw((1, BRANCH_W)), cw((SGU_G, SGU_TILE, SGU_TILE)), cw((SGU_TILE, BRANCH_W)),
        cw((N_BRANCH, BRANCH_W, D_MODEL)), cw((D_MODEL, D_MODEL)),
        _const_spec((N_EXPERTS, d), lambda bb, j: (0, 0)),
        _const_spec((N_EXPERTS, 1), lambda bb, j: (0, 0)),
    ]
    assert len(in_specs) == N_MIX_INPUTS
    args = [xs, xs, xs, mod, lw["g1"], lw["g2"], lw["w_h"], lw["w_m"], attn,
            lw["conv_w"], lw["pool_w"], lw["pool_scale"], lw["g_sgu"], lw["sgu_w"], lw["sgu_b"],
            lw["w_branch"], lw["w_o"], router_w, router_b]
    aliases = {}
    if carry is not None:
        in_specs += [pl.BlockSpec(memory_space=pl.ANY)] * len(carry)
        args += list(carry)
        aliases = {N_MIX_INPUTS + n: n for n in range(len(carry))}
    return pl.pallas_call(
        functools.partial(_mix_kernel, seg_tiles=seg_tiles),
        grid=(b, seg_tiles),
        in_specs=in_specs,
        out_specs=[
            pl.BlockSpec((None, tm, d), tok),
            pl.BlockSpec((None, tm, d), tok),
            pl.BlockSpec((None, tm, LANES), tok),
            pl.BlockSpec((None, 2 * N_EXPERT_GROUPS, tm), lambda bb, j: (bb, 0, base + j)),
        ],
        out_shape=[
            jax.ShapeDtypeStruct((b, out_len, d), F32),
            jax.ShapeDtypeStruct((b, out_len, d), BF16),
            jax.ShapeDtypeStruct((b, out_len, LANES), F32),
            jax.ShapeDtypeStruct((b, 2 * N_EXPERT_GROUPS, out_len), F32),
        ],
        input_output_aliases=aliases,
        scratch_shapes=[pltpu.VMEM((tm + 2 * HALO, d), BF16)],
        compiler_params=pltpu.CompilerParams(
            dimension_semantics=("arbitrary", "arbitrary"), vmem_limit_bytes=VMEM_LIMIT),
    )(*args)


def _moe_kernel(x_ref, h_ref, cmbg_ref, cmb_ref, grp_ref, mod_ref, before_ref, after_ref,
                w1_ref, w3_ref, w2_ref, o_ref, rank_col, rank_row, count, *, chunk, lat_rows):
    tile = x_ref.shape[0]
    g = pl.program_id(2)
    n_exp = w1_ref.shape[0]

    @pl.when(g == 0)
    def _():
        o_ref[...] = x_ref[...]
        lane = lax.broadcasted_iota(jnp.int32, (1, LANES), 1)
        is_group = jnp.logical_and(lane >= N_EXPERTS, lane < N_EXPERTS + N_EXPERT_GROUPS)
        member_c = jnp.where(is_group, cmb_ref[...], 0.0)
        member_r = grp_ref[...]
        earlier_c = _dot(before_ref[...], member_c.astype(BF16))
        earlier_r = _dot(member_r.astype(BF16), after_ref[...])
        for gg in range(N_EXPERT_GROUPS):
            col = slice(N_EXPERTS + gg, N_EXPERTS + gg + 1)
            rank_col[gg] = jnp.broadcast_to(
                jnp.where(member_c[:, col] > 0.0, earlier_c[:, col], -1.0), (tile, LANES))
            rank_row[gg] = jnp.broadcast_to(
                jnp.where(member_r[gg:gg + 1] > 0.0, earlier_r[gg:gg + 1], -1.0), rank_row.shape[1:])
            count[gg] = jnp.sum(member_r[gg:gg + 1]).astype(jnp.int32)

    gate = mod_ref[1, N_MOD - 1:N_MOD]
    if lat_rows is not None:
        pos = pl.program_id(1) * tile + lax.broadcasted_iota(jnp.int32, (tile, 1), 0)
        gate = jnp.where(pos < lat_rows, gate, mod_ref[0, N_MOD - 1:N_MOD])
    cg = cmbg_ref[...]
    cg_hi = cg.astype(BF16)
    cg_lo = (cg - cg_hi.astype(F32)).astype(BF16)
    rrow = rank_row[g][0:1, :]
    rcol = rank_col[g]
    rcol = jnp.concatenate([rcol] * (chunk // LANES), axis=1)

    def one_chunk(c, carry):
        first = (c * chunk).astype(F32)
        want_r = lax.broadcasted_iota(jnp.int32, (chunk, 1), 0).astype(F32) + first
        gather = jnp.where(rrow == want_r, 1.0, 0.0).astype(BF16)
        h = _dot(gather, h_ref[...]).astype(BF16)
        cm = _dot(gather, cg_hi) + _dot(gather, cg_lo)
        acts = []
        for e in range(n_exp):
            a = _dot(h, w1_ref[e])
            a = (a * jax.nn.sigmoid(a)) * _dot(h, w3_ref[e])
            acts.append((a * cm[:, e:e + 1]).astype(BF16))
        y = _dot(jnp.concatenate(acts, axis=-1), w2_ref[...]).astype(BF16)
        want_c = lax.broadcasted_iota(jnp.int32, (1, chunk), 1).astype(F32) + first
        scatter = jnp.where(rcol == want_c, 1.0, 0.0).astype(BF16)
        o_ref[...] += gate * _dot(scatter, y)
        return carry

    lax.fori_loop(0, (count[g] + chunk - 1) // chunk, one_chunk, 0)


def _moe_tile(s):
    for tile, chunk in ((768, 256), (1024, 384), (512, 256), (256, 128)):
        if s % tile == 0:
            return tile, chunk
    raise ValueError(f"unsupported sequence length {s}")


def _moe(x_new, h2, cmb, grp, mod, lw, i, lat_rows):
    b, s, d = x_new.shape
    tile, chunk = _moe_tile(s)
    ge = EXPERTS_PER_GROUP
    cmbg = jnp.moveaxis(cmb[..., :N_EXPERTS].reshape(b, s, N_EXPERT_GROUPS, ge), 2, 0)
    tpos = jnp.arange(tile)
    before = (tpos[None, :] < tpos[:, None]).astype(BF16)
    tok = lambda bb, j, e: (bb, j, 0)
    kern = functools.partial(_moe_kernel, chunk=chunk, lat_rows=lat_rows)
    return pl.pallas_call(
        kern,
        grid=(b, s // tile, N_EXPERT_GROUPS),
        in_specs=[
            pl.BlockSpec((None, tile, d), tok),
            pl.BlockSpec((None, tile, d), tok),
            pl.BlockSpec((None, None, tile, ge), lambda bb, j, e: (e, bb, j, 0)),
            pl.BlockSpec((None, tile, LANES), tok),
            pl.BlockSpec((None, 2 * N_EXPERT_GROUPS, tile), lambda bb, j, e: (bb, 0, j)),
            pl.BlockSpec((None, None, 2, N_MOD, d), lambda bb, j, e: (i, bb, 0, 0, 0)),
            _const_spec((tile, tile), lambda bb, j, e: (0, 0)),
            _const_spec((tile, tile), lambda bb, j, e: (0, 0)),
            pl.BlockSpec((None, ge, d, EXPERT_FF), lambda bb, j, e: (i, e, 0, 0)),
            pl.BlockSpec((None, ge, d, EXPERT_FF), lambda bb, j, e: (i, e, 0, 0)),
            pl.BlockSpec((None, ge * EXPERT_FF, d), lambda bb, j, e: (i, e, 0)),
        ],
        out_specs=pl.BlockSpec((None, tile, d), tok),
        out_shape=jax.ShapeDtypeStruct((b, s, d), F32),
        input_output_aliases={0: 0},
        scratch_shapes=[
            pltpu.VMEM((N_EXPERT_GROUPS, tile, LANES), F32),
            pltpu.VMEM((N_EXPERT_GROUPS, 8, tile), F32),
            pltpu.SMEM((N_EXPERT_GROUPS,), jnp.int32),
        ],
        compiler_params=pltpu.CompilerParams(
            dimension_semantics=("arbitrary", "arbitrary", "arbitrary"), vmem_limit_bytes=VMEM_LIMIT),
    )(x_new, h2, cmbg, cmb, grp, mod, before, before.T, lw["w1"], lw["w3"], lw["w2"])


def _prepare(p):
    dpt = p["w_in"].shape[0]
    w_in = p["w_in"]
    zeros = lambda n: jnp.zeros((dpt, D_MODEL, n), F32)
    kr_mid = KR_OFF + ROPE_HALF
    w_a = jnp.concatenate([
        w_in[..., KV_OFF:KR_OFF],
        zeros(NOPE_DIM), w_in[..., KR_OFF:Q_OFF], zeros(LANES - QK_DIM),
        zeros(NOPE_DIM), w_in[..., kr_mid:Q_OFF], w_in[..., KR_OFF:kr_mid], zeros(LANES - QK_DIM),
        w_in[..., Q_OFF:CONV_OFF]], axis=-1)
    gb_lo, gc_lo, u_lo = CONV_OFF, CONV_OFF + BRANCH_W, CONV_OFF + 2 * BRANCH_W
    w_h = jnp.concatenate([w_in[..., gc_lo:u_lo], w_in[..., u_lo:POOL_OFF],
                           w_in[..., POOL_OFF:SGU_OFF]], axis=-1)
    w_m = jnp.concatenate([w_in[..., gb_lo:gc_lo], w_in[..., SGU_OFF:]], axis=-1)
    w_ukv = p["w_ukv"]
    w_uk = jnp.concatenate([w_ukv[..., :NOPE_DIM], jnp.zeros_like(w_ukv[..., :NOPE_DIM])], axis=-1)
    w_v = w_ukv[..., NOPE_DIM:]
    even = (jnp.arange(N_HEADS) % 2 == 0)[None, None, :, None]
    w_uv = jnp.concatenate([jnp.where(even, w_v, 0.0), jnp.where(even, 0.0, w_v)], axis=-1)
    w_q = p["w_uq"]
    w_uq = jnp.pad(w_q, ((0, 0), (0, 0), (0, 0), (0, LANES - QK_DIM)))
    rope_mid = NOPE_DIM + ROPE_HALF
    w_uqs = jnp.concatenate([
        jnp.zeros_like(w_q[..., :NOPE_DIM]), w_q[..., rope_mid:], w_q[..., NOPE_DIM:rope_mid],
        jnp.zeros_like(w_q[..., :LANES - QK_DIM])], axis=-1)
    eye = jnp.eye(SGU_TILE // SGU_CHUNK, dtype=F32)
    sgu_bd = jnp.einsum("ab,lgpq->lgapbq", eye, p["sgu_w"]).reshape(dpt, SGU_G, SGU_TILE, SGU_TILE)
    sgu_b = jnp.repeat(jnp.swapaxes(p["sgu_b"], 1, 2), SGU_CHUNK, axis=2)
    sgu_b = jnp.tile(sgu_b, (1, SGU_TILE // SGU_CHUNK, 1))
    return {
        "g1": p["norm1_g"][:, None, :], "g2": p["norm2_g"][:, None, :],
        "w_a": w_a.astype(BF16), "w_h": w_h.astype(BF16), "w_m": w_m.astype(BF16),
        "g_kv": p["kv_a_norm_g"][:, None, :], "g_qa": p["q_a_norm_g"][:, None, :],
        "w_uk": w_uk.reshape(dpt, KV_LORA, N_HEADS * LANES).astype(BF16),
        "w_uv": w_uv.reshape(dpt, KV_LORA, N_HEADS * LANES).astype(BF16),
        "w_uq": w_uq.reshape(dpt, Q_LORA, N_HEADS * LANES).astype(BF16),
        "w_uqs": w_uqs.reshape(dpt, Q_LORA, N_HEADS * LANES).astype(BF16),
        "conv_w": p["conv_w"], "pool_w": p["pool_w"].astype(BF16),
        "pool_scale": p["pool_scale"][:, None, :], "g_sgu": p["sgu_norm_g"][:, None, :],
        "sgu_w": sgu_bd.astype(BF16), "sgu_b": sgu_b,
        "w_branch": p["w_branch"].astype(BF16), "w_o": p["w_o"].astype(BF16),
        "w1": p["moe_w1"].astype(BF16), "w3": p["moe_w3"].astype(BF16),
        "w2": p["moe_w2"].reshape(dpt, N_EXPERTS * EXPERT_FF, D_MODEL).astype(BF16),
    }


def _rope_tables(ctx_len, length, k_gain, q_gain):
    rows = length // GRID_W
    t = jnp.arange(rows * GRID_W)
    grid = jnp.stack([t // GRID_W, t % GRID_W], -1).astype(F32)
    inv_freq = ROPE_THETA ** (-jnp.arange(AXIS_PAIRS, dtype=F32) / AXIS_PAIRS)
    ang = jnp.concatenate([grid[:, :1] * inv_freq, grid[:, 1:] * inv_freq], -1)
    cos, sin = jnp.cos(ang), jnp.sin(ang)
    one = jnp.ones((length, NOPE_DIM), F32)
    tail = jnp.ones((length, LANES - QK_DIM), F32)
    zn = jnp.zeros((length, NOPE_DIM), F32)
    zt = jnp.zeros((length, LANES - QK_DIM), F32)
    rc = jnp.concatenate([one, cos, cos, tail], -1)
    rs = jnp.concatenate([zn, -sin, sin, zt], -1)
    ctx_id = lambda fill: jnp.full((ctx_len, LANES), fill, F32)
    rc = jnp.concatenate([rc, ctx_id(1.0)], 0)
    rs = jnp.concatenate([rs, ctx_id(0.0)], 0)

    def tables(gain, const):
        g = jnp.pad(gain, ((0, 0), (0, LANES - QK_DIM)))
        mid = NOPE_DIM + ROPE_HALF
        g_swap = jnp.concatenate([g[:, :NOPE_DIM], g[:, mid:QK_DIM], g[:, NOPE_DIM:mid],
                                  g[:, QK_DIM:]], -1)
        return (g[:, None, :] * rc[None]) * const, (g_swap[:, None, :] * rs[None]) * const

    gak, gbk = tables(k_gain, 1.0)
    gaq, gbq = tables(q_gain, ATTN_SCALE * LOG2E)
    return gak, gbk, gaq, gbq


def kernel(x, c, ctx, c_ctx, mod_w, mod_b, norm1_g, norm2_g, w_in, kv_a_norm_g, w_ukv, k_norm_g,
           q_a_norm_g, w_uq, q_norm_g, conv_w, pool_w, pool_scale, sgu_norm_g, sgu_w, sgu_b,
           w_branch, w_o, router_w, router_b, moe_w1, moe_w3, moe_w2):
    b, length, d = x.shape
    ctx_len = ctx.shape[1]
    assert d == D_MODEL and length % TOKEN_TILE == 0 and ctx_len % TOKEN_TILE == 0
    assert length % GRID_W == 0
    s = length + ctx_len

    lw = _prepare(dict(
        norm1_g=norm1_g, norm2_g=norm2_g, w_in=w_in, kv_a_norm_g=kv_a_norm_g, w_ukv=w_ukv,
        k_norm_g=k_norm_g, q_a_norm_g=q_a_norm_g, w_uq=w_uq, q_norm_g=q_norm_g, conv_w=conv_w,
        pool_w=pool_w, pool_scale=pool_scale, sgu_norm_g=sgu_norm_g, sgu_w=sgu_w, sgu_b=sgu_b,
        w_branch=w_branch, w_o=w_o, moe_w1=moe_w1, moe_w3=moe_w3, moe_w2=moe_w2))
    rope_tabs = _rope_tables(ctx_len, length, k_norm_g, q_norm_g)
    rw = router_w.T
    rb = router_b[:, None]

    rows = -(-(b + 1) // 8) * 8
    cvec = jnp.concatenate([c, c_ctx[None, :], jnp.zeros((rows - b - 1, d), F32)], 0)
    mod = _modulation(cvec, mod_w, mod_b).reshape(DEPTH, rows, N_MOD, d)
    mod = jnp.stack([jnp.broadcast_to(mod[:, b:b + 1], (DEPTH, b, N_MOD, d)), mod[:, :b]], axis=2)

    xs = jnp.concatenate([x, ctx], axis=1)
    for i in range(DEPTH):
        last = i == DEPTH - 1
        q, k, v = _qkv(xs, mod, lw, i, rope_tabs, length)
        attn = _attention(q, k, v, length, not last)
        out_len = length if last else s
        outs = _mix(xs, mod, attn, lw, i, 0, length, 1, out_len, None, rw, rb)
        if not last:
            outs = _mix(xs, mod, attn, lw, i, length, ctx_len, 0, out_len, outs, rw, rb)
        x_new, h2, cmb, grp = outs
        xs = _moe(x_new, h2, cmb, grp, mod, lw, i, None if last else length)
    return xs
```

```python
import functools

import jax
import jax.numpy as jnp
from jax import lax
from jax.experimental import pallas as pl
from jax.experimental.pallas import tpu as pltpu

F32 = jnp.float32
BF16 = jnp.bfloat16

D_MODEL = 1024
DEPTH = 4
GRID_W = 64
EPS = 1e-6
N_MOD = 6
N_BRANCH = 4
BRANCH_W = 512
N_HEADS = 8
Q_LORA = 256
KV_LORA = 128
NOPE_DIM = 64
ROPE_DIM = 32
V_DIM = 64
QK_DIM = NOPE_DIM + ROPE_DIM
ROPE_HALF = ROPE_DIM // 2
AXIS_PAIRS = ROPE_DIM // 4
ROPE_THETA = 10000.0
ATTN_SCALE = QK_DIM ** -0.5
LOG2E = 1.4426950408889634
CONV_K = 3
POOL_WINDOWS = (2, 4, 8, 16)
POOL_GC = 128
SGU_CHUNK = 128
SGU_G = 4
N_EXPERTS = 16
N_EXPERT_GROUPS = 4
EXPERTS_PER_GROUP = 4
EXPERT_FF = 256

KV_OFF = 0
KR_OFF = KV_OFF + KV_LORA
Q_OFF = KR_OFF + ROPE_DIM
CONV_OFF = Q_OFF + Q_LORA
POOL_OFF = CONV_OFF + 3 * BRANCH_W
SGU_OFF = POOL_OFF + BRANCH_W
GATE_OFF = SGU_OFF + 2 * BRANCH_W

LANES = 128
HALO = 16
TOKEN_TILE = 256
SGU_TILE = 256
MIX_TILES = (512, 256)
ATTN_TILES = (512, 256)
VMEM_LIMIT = 56 * 1024 * 1024


def _rms(x, width=None):
    width = x.shape[-1] if width is None else width
    return x * lax.rsqrt(jnp.sum(x * x, axis=-1, keepdims=True) * (1.0 / width) + EPS)


def _norm_mod(x, g, shift, scale):
    return (_rms(x) * g) * (1.0 + scale) + shift


def _dot(a, b):
    return jnp.dot(a, b, preferred_element_type=F32)


def _const_spec(shape, index_map):
    return pl.BlockSpec(shape, index_map, pipeline_mode=pl.Buffered(1))


def _mod_kernel(c_ref, w_ref, b_ref, o_ref):
    c = c_ref[...]
    sc = c * jax.nn.sigmoid(c)
    o_ref[...] = jnp.dot(sc, w_ref[...], preferred_element_type=F32,
                         precision=lax.Precision.HIGHEST) + b_ref[...]


def _modulation(cvec, mod_w, mod_b):
    rows = cvec.shape[0]
    n = mod_w.shape[-1]
    bn = 1536
    return pl.pallas_call(
        _mod_kernel,
        grid=(DEPTH, n // bn),
        in_specs=[
            pl.BlockSpec((rows, D_MODEL), lambda i, j: (0, 0)),
            pl.BlockSpec((None, D_MODEL, bn), lambda i, j: (i, 0, j)),
            pl.BlockSpec((None, 1, bn), lambda i, j: (i, 0, j)),
        ],
        out_specs=pl.BlockSpec((None, rows, bn), lambda i, j: (i, 0, j)),
        out_shape=jax.ShapeDtypeStruct((DEPTH, rows, n), F32),
        compiler_params=pltpu.CompilerParams(
            dimension_semantics=("arbitrary", "arbitrary"), vmem_limit_bytes=VMEM_LIMIT),
    )(cvec, mod_w, mod_b.reshape(DEPTH, 1, n))


def _qkv_kernel(x_ref, mod_ref, g1_ref, wa_ref, gkv_ref, gq_ref, wuk_ref, wuv_ref, wuq_ref, wuqs_ref,
                gak_ref, gbk_ref, gaq_ref, gbq_ref, q_ref, k_ref, v_ref):
    x = x_ref[...]
    m = mod_ref[...]
    h = _norm_mod(x, g1_ref[...], m[0:1], m[1:2]).astype(BF16)
    p = _dot(h, wa_ref[...])
    ckv = (_rms(p[:, :KV_LORA]) * gkv_ref[...]).astype(BF16)
    kk = _dot(ckv, wuk_ref[...])
    vv = _dot(ckv, wuv_ref[...])
    kr = p[:, LANES:2 * LANES]
    krs = p[:, 2 * LANES:3 * LANES]
    cq = (_rms(p[:, 3 * LANES:]) * gq_ref[...]).astype(BF16)
    qq = _dot(cq, wuq_ref[...])
    qs = _dot(cq, wuqs_ref[...])
    gak, gbk, gaq, gbq = gak_ref[...], gbk_ref[...], gaq_ref[...], gbq_ref[...]
    k_shared = kr * gak + krs * gbk
    kr_sq = jnp.sum(kr * kr, axis=-1, keepdims=True)
    lane = lax.broadcasted_iota(jnp.int32, (1, LANES), 1)
    one_col = [jnp.where(lane == V_DIM, 1.0, 0.0), jnp.where(lane == 0, 1.0, 0.0)]

    for hd in range(N_HEADS):
        cols = slice(hd * LANES, (hd + 1) * LANES)
        kn = kk[:, cols]
        rk = lax.rsqrt((jnp.sum(kn * kn, axis=-1, keepdims=True) + kr_sq) * (1.0 / QK_DIM) + EPS)
        k_ref[hd] = ((kn * gak + k_shared) * rk).astype(BF16)
        qh = qq[:, cols]
        rq = lax.rsqrt(jnp.sum(qh * qh, axis=-1, keepdims=True) * (1.0 / QK_DIM) + EPS)
        q_ref[hd] = ((qh * gaq + qs[:, cols] * gbq) * rq).astype(BF16)
        v_ref[hd] = (vv[:, cols] + one_col[hd % 2]).astype(BF16)


def _qkv(xs, mod, lw, i, rope_tabs, lat_len):
    b, s, d = xs.shape
    tm = TOKEN_TILE
    n_lat_tiles = lat_len // tm

    def cw(shape):
        return _const_spec((None,) + shape, lambda bb, j: (i,) + (0,) * len(shape))

    head = lambda bb, j: (bb, 0, j, 0)
    rspec = pl.BlockSpec((None, tm, LANES), lambda bb, j: (i, j, 0))
    hspec = pl.BlockSpec((None, N_HEADS, tm, LANES), head)
    hshape = jax.ShapeDtypeStruct((b, N_HEADS, s, LANES), BF16)
    return pl.pallas_call(
        _qkv_kernel,
        grid=(b, s // tm),
        in_specs=[
            pl.BlockSpec((None, tm, d), lambda bb, j: (bb, j, 0)),
            pl.BlockSpec((None, None, None, N_MOD, d),
                         lambda bb, j: (i, bb, jnp.where(j < n_lat_tiles, 1, 0), 0, 0)),
            cw((1, d)), cw((d, 5 * LANES)), cw((1, KV_LORA)), cw((1, Q_LORA)),
            cw((KV_LORA, N_HEADS * LANES)), cw((KV_LORA, N_HEADS * LANES)),
            cw((Q_LORA, N_HEADS * LANES)), cw((Q_LORA, N_HEADS * LANES)),
            rspec, rspec, rspec, rspec,
        ],
        out_specs=[hspec, hspec, hspec],
        out_shape=[hshape, hshape, hshape],
        compiler_params=pltpu.CompilerParams(
            dimension_semantics=("arbitrary", "arbitrary"), vmem_limit_bytes=VMEM_LIMIT),
    )(xs, mod, lw["g1"], lw["w_a"], lw["g_kv"], lw["g_qa"], lw["w_uk"], lw["w_uv"], lw["w_uq"],
      lw["w_uqs"], *rope_tabs)


def _attn_kernel(*refs, key_lo):
    q_ref, k_ref, v_ref = refs[:3]
    o_ref = refs[-1]
    n_keys = k_ref.shape[1]
    lane = lax.broadcasted_iota(jnp.int32, (1, LANES), 1)
    for pair in range(N_HEADS // 2):
        outs = []
        for hd in (2 * pair, 2 * pair + 1):
            s = lax.dot_general(q_ref[hd], k_ref[hd, key_lo:n_keys, :], (((1,), (1,)), ((), ())),
                                preferred_element_type=F32)
            e = jnp.exp2(s - jnp.max(s, axis=-1, keepdims=True))
            o = _dot(e.astype(BF16), v_ref[hd, key_lo:n_keys, :])
            den = o[:, V_DIM:V_DIM + 1] if hd % 2 == 0 else o[:, 0:1]
            outs.append(o / den)
        o_ref[:, pair * LANES:(pair + 1) * LANES] = jnp.where(
            lane < V_DIM, outs[0], outs[1]).astype(BF16)


def _attention(q, k, v, q_start, q_len, key_lo, carry):
    b, _, s, _ = q.shape
    tq = next(t for t in ATTN_TILES if q_len % t == 0 and q_start % t == 0)
    base = q_start // tq
    in_specs = [
        pl.BlockSpec((None, N_HEADS, tq, LANES), lambda bb, j: (bb, 0, base + j, 0)),
        pl.BlockSpec((None, N_HEADS, s, LANES), lambda bb, j: (bb, 0, 0, 0)),
        pl.BlockSpec((None, N_HEADS, s, LANES), lambda bb, j: (bb, 0, 0, 0)),
    ]
    args = [q, k, v]
    aliases = {}
    if carry is not None:
        in_specs.append(pl.BlockSpec(memory_space=pl.ANY))
        args.append(carry)
        aliases = {3: 0}
    return pl.pallas_call(
        functools.partial(_attn_kernel, key_lo=key_lo),
        grid=(b, q_len // tq),
        in_specs=in_specs,
        out_specs=pl.BlockSpec((None, tq, BRANCH_W), lambda bb, j: (bb, base + j, 0)),
        out_shape=jax.ShapeDtypeStruct((b, s, BRANCH_W), BF16),
        input_output_aliases=aliases,
        compiler_params=pltpu.CompilerParams(
            dimension_semantics=("arbitrary", "arbitrary"), vmem_limit_bytes=VMEM_LIMIT),
    )(*args)


def _route(logits, rbias):
    row = lax.broadcasted_iota(jnp.int32, logits.shape, 0)
    slot = row % EXPERTS_PER_GROUP
    s = jax.nn.sigmoid(logits)
    sb = s + rbias

    def mate(x, dd):
        fwd = pltpu.roll(x, N_EXPERTS - dd, 0)
        back = pltpu.roll(x, EXPERTS_PER_GROUP - dd, 0)
        return jnp.where(slot + dd < EXPERTS_PER_GROUP, fwd, back)

    mates = [mate(sb, dd) for dd in (1, 2, 3)]
    vals = [sb] + mates
    gscore = None
    for a in range(4):
        for c in range(a + 1, 4):
            pair = vals[a] + vals[c]
            gscore = pair if gscore is None else jnp.maximum(gscore, pair)
    group_ok = None
    for gg in (1, 2, 3):
        other = pltpu.roll(gscore, EXPERTS_PER_GROUP * gg, 0)
        other_is_lower = row >= EXPERTS_PER_GROUP * gg
        wins = jnp.logical_or(gscore > other,
                              jnp.logical_and(gscore == other, jnp.logical_not(other_is_lower)))
        group_ok = wins if group_ok is None else jnp.logical_and(group_ok, wins)
    rank = jnp.zeros(logits.shape, jnp.int32)
    for dd, mv in zip((1, 2, 3), mates):
        wrapped = slot + dd >= EXPERTS_PER_GROUP
        beats = jnp.logical_or(mv > sb, jnp.logical_and(wrapped, mv == sb))
        rank = rank + beats.astype(jnp.int32)
    chosen = jnp.logical_and(rank < 2, group_ok)
    w = jnp.where(chosen, s, 0.0)
    group_f = jnp.where(group_ok, 1.0, 0.0)
    group_onehot = jnp.concatenate(
        [group_f[gg * EXPERTS_PER_GROUP:gg * EXPERTS_PER_GROUP + 1]
         for gg in range(N_EXPERT_GROUPS)], axis=0)
    return w / jnp.sum(w, axis=0, keepdims=True), group_onehot


N_MIX_INPUTS = 19


def _mix_kernel(*refs, seg_tiles):
    (x_ref, xp_ref, xn_ref, mod_ref, g1_ref, g2_ref, wh_ref, wm_ref, attn_ref,
     convw_ref, poolw_ref, pscale_ref, gsgu_ref, sguw_ref, sgub_ref, wbr_ref, wo_ref,
     rw_ref, rb_ref) = refs[:N_MIX_INPUTS]
    xo_ref, h2_ref, cmb_ref, grp_ref, hcat_ref = refs[-5:]
    tm = x_ref.shape[0]
    j = pl.program_id(1)
    m = mod_ref[...]
    g1 = g1_ref[...]
    x = x_ref[...]

    def hnorm(xv):
        return _norm_mod(xv, g1, m[0:1], m[1:2]).astype(BF16)

    hm = hnorm(x)
    hcat_ref[0:HALO, :] = hnorm(xp_ref[...])
    hcat_ref[HALO:HALO + tm, :] = hm
    hcat_ref[HALO + tm:, :] = hnorm(xn_ref[...])

    nrow = tm + 2 * HALO
    row = lax.broadcasted_iota(jnp.int32, (nrow, 1), 0)
    keep = jnp.logical_and(jnp.logical_or(row >= HALO, j > 0),
                           jnp.logical_or(row < HALO + tm, j < seg_tiles - 1))
    ph = _dot(hcat_ref[...], wh_ref[...])
    ph = jnp.where(keep, ph, 0.0)

    def shifted(a, k):
        return pltpu.roll(a, (-k) % nrow, 0)[HALO:HALO + tm]

    cu = ph[:, :BRANCH_W] * ph[:, BRANCH_W:2 * BRANCH_W]
    cw = convw_ref[...]
    conv = shifted(cu, -1) * cw[0:1] + cu[HALO:HALO + tm] * cw[1:2] + shifted(cu, 1) * cw[2:3]
    gb = _dot(hm, wm_ref[:, 0:BRANCH_W])
    br_conv = gb * conv

    pp = ph[:, 2 * BRANCH_W:]
    seg_len = seg_tiles * tm
    pos = j * tm + lax.broadcasted_iota(jnp.int32, (tm, 1), 0)
    level = pp + pltpu.roll(pp, 1, 0)
    pooled = []
    for gi, win in enumerate(POOL_WINDOWS):
        if gi > 0:
            half = win // 4
            level = level[:, POOL_GC:]
            level = pltpu.roll(level, nrow - half, 0) + pltpu.roll(level, half, 0)
        cnt = (jnp.minimum(pos + win // 2, seg_len) - jnp.maximum(pos - win // 2, 0)).astype(F32)
        own = pp[HALO:HALO + tm, gi * POOL_GC:(gi + 1) * POOL_GC]
        pooled.append(level[HALO:HALO + tm, :POOL_GC] / cnt - own)
    br_pool = jnp.concatenate(
        [_dot(pooled[gi].astype(BF16), poolw_ref[gi]) for gi in range(len(POOL_WINDOWS))], axis=-1)
    br_pool = br_pool * pscale_ref[...]

    z = jax.nn.gelu(_dot(hm, wm_ref[:, BRANCH_W:3 * BRANCH_W]), approximate=True)
    zu = z[:, :BRANCH_W]
    zv = (_rms(z[:, BRANCH_W:]) * gsgu_ref[...]).astype(BF16)
    mixed = jnp.concatenate([
        jnp.concatenate([_dot(sguw_ref[gi], zv[r0:r0 + SGU_TILE, gi * LANES:(gi + 1) * LANES])
                         for gi in range(SGU_G)], axis=-1) + sgub_ref[...]
        for r0 in range(0, tm, SGU_TILE)], axis=0)
    br_sgu = zu * mixed

    branches = (attn_ref[...], br_conv.astype(BF16), br_pool.astype(BF16), br_sgu.astype(BF16))
    merged = None
    for nb, br in enumerate(branches):
        lo = 3 * BRANCH_W + nb * D_MODEL
        gate = jax.nn.sigmoid(_dot(hm, wm_ref[:, lo:lo + D_MODEL]))
        term = gate * _dot(br, wbr_ref[nb])
        merged = term if merged is None else merged + term
    x_new = x + m[2:3] * _dot(merged.astype(BF16), wo_ref[...])
    xo_ref[...] = x_new

    h2 = _norm_mod(x_new, g2_ref[...], m[3:4], m[4:5])
    h2_ref[...] = h2.astype(BF16)
    logits = lax.dot_general(rw_ref[...], h2, (((1,), (1,)), ((), ())),
                             preferred_element_type=F32, precision=lax.Precision.HIGHEST)
    cmb_t, grp_t = _route(logits, rb_ref[...])
    pad = jnp.zeros((LANES - N_EXPERTS - N_EXPERT_GROUPS, tm), F32)
    cmb_ref[...] = jnp.concatenate([cmb_t, grp_t, pad], axis=0).T
    grp_ref[...] = jnp.concatenate([grp_t, jnp.zeros_like(grp_t)], axis=0)


def _mix(xs, mod, attn, lw, i, seg_start, seg_len, seg_id, out_len, carry, router_w, router_b):
    b, s, d = xs.shape
    tm = next(t for t in MIX_TILES if seg_len % t == 0 and seg_start % t == 0)
    seg_tiles = seg_len // tm
    base = seg_start // tm
    hb = tm // HALO
    base_h = seg_start // HALO
    last_h = s // HALO - 1

    def cw(shape):
        return _const_spec((None,) + shape, lambda bb, j: (i,) + (0,) * len(shape))

    tok = lambda bb, j: (bb, base + j, 0)
    in_specs = [
        pl.BlockSpec((None, tm, d), tok),
        pl.BlockSpec((None, HALO, d), lambda bb, j: (bb, jnp.maximum(base_h + j * hb - 1, 0), 0)),
        pl.BlockSpec((None, HALO, d),
                     lambda bb, j: (bb, jnp.minimum(base_h + (j + 1) * hb, last_h), 0)),
        pl.BlockSpec((None, None, None, N_MOD, d), lambda bb, j: (i, bb, seg_id, 0, 0)),
        cw((1, d)), cw((1, d)),
        cw((d, 3 * BRANCH_W)), cw((d, 3 * BRANCH_W + N_BRANCH * D_MODEL)),
        pl.BlockSpec((None, tm, BRANCH_W), tok),
        cw((CONV_K, BRANCH_W)), cw((len(POOL_WINDOWS), POOL_GC, POOL_GC)), cw((1, BRANCH_W)),
        cw((1, BRANCH_W)), cw((SGU_G, SGU_TILE, SGU_TILE)), cw((SGU_TILE, BRANCH_W)),
        cw((N_BRANCH, BRANCH_W, D_MODEL)), cw((D_MODEL, D_MODEL)),
        _const_spec((N_EXPERTS, d), lambda bb, j: (0, 0)),
        _const_spec((N_EXPERTS, 1), lambda bb, j: (0, 0)),
    ]
    assert len(in_specs) == N_MIX_INPUTS
    args = [xs, xs, xs, mod, lw["g1"], lw["g2"], lw["w_h"], lw["w_m"], attn,
            lw["conv_w"], lw["pool_w"], lw["pool_scale"], lw["g_sgu"], lw["sgu_w"], lw["sgu_b"],
            lw["w_branch"], lw["w_o"], router_w, router_b]
    aliases = {}
    if carry is not None:
        in_specs += [pl.BlockSpec(memory_space=pl.ANY)] * len(carry)
        args += list(carry)
        aliases = {N_MIX_INPUTS + n: n for n in range(len(carry))}
    return pl.pallas_call(
        functools.partial(_mix_kernel, seg_tiles=seg_tiles),
        grid=(b, seg_tiles),
        in_specs=in_specs,
        out_specs=[
            pl.BlockSpec((None, tm, d), tok),
            pl.BlockSpec((None, tm, d), tok),
            pl.BlockSpec((None, tm, LANES), tok),
            pl.BlockSpec((None, 2 * N_EXPERT_GROUPS, tm), lambda bb, j: (bb, 0, base + j)),
        ],
        out_shape=[
            jax.ShapeDtypeStruct((b, out_len, d), F32),
            jax.ShapeDtypeStruct((b, out_len, d), BF16),
            jax.ShapeDtypeStruct((b, out_len, LANES), F32),
            jax.ShapeDtypeStruct((b, 2 * N_EXPERT_GROUPS, out_len), F32),
        ],
        input_output_aliases=aliases,
        scratch_shapes=[pltpu.VMEM((tm + 2 * HALO, d), BF16)],
        compiler_params=pltpu.CompilerParams(
            dimension_semantics=("arbitrary", "arbitrary"), vmem_limit_bytes=VMEM_LIMIT),
    )(*args)


def _moe_kernel(x_ref, h_ref, cmb_ref, grp_ref, mod_ref, before_ref, after_ref,
                w1_ref, w3_ref, w2_ref, o_ref, rank_col, rank_row, group_cmb, count, *, chunk, lat_rows):
    tile = x_ref.shape[0]
    g = pl.program_id(2)
    n_exp = w1_ref.shape[0]

    @pl.when(g == 0)
    def _():
        o_ref[...] = x_ref[...]
        lane = lax.broadcasted_iota(jnp.int32, (1, LANES), 1)
        is_group = jnp.logical_and(lane >= N_EXPERTS, lane < N_EXPERTS + N_EXPERT_GROUPS)
        member_c = jnp.where(is_group, cmb_ref[...], 0.0)
        member_r = grp_ref[...]
        earlier_c = _dot(before_ref[...], member_c.astype(BF16))
        earlier_r = _dot(member_r.astype(BF16), after_ref[...])
        for gg in range(N_EXPERT_GROUPS):
            col = slice(N_EXPERTS + gg, N_EXPERTS + gg + 1)
            rank_col[gg] = jnp.broadcast_to(
                jnp.where(member_c[:, col] > 0.0, earlier_c[:, col], -1.0), (tile, LANES))
            rank_row[gg] = jnp.broadcast_to(
                jnp.where(member_r[gg:gg + 1] > 0.0, earlier_r[gg:gg + 1], -1.0), rank_row.shape[1:])
            group_cmb[gg] = cmb_ref[:, gg * EXPERTS_PER_GROUP:(gg + 1) * EXPERTS_PER_GROUP]
            count[gg] = jnp.sum(member_r[gg:gg + 1]).astype(jnp.int32)

    gate = mod_ref[1, N_MOD - 1:N_MOD]
    if lat_rows is not None:
        pos = pl.program_id(1) * tile + lax.broadcasted_iota(jnp.int32, (tile, 1), 0)
        gate = jnp.where(pos < lat_rows, gate, mod_ref[0, N_MOD - 1:N_MOD])
    cg = group_cmb[g]
    cg_hi = cg.astype(BF16)
    cg_lo = (cg - cg_hi.astype(F32)).astype(BF16)
    rrow = rank_row[g][0:1, :]
    rcol = rank_col[g]
    rcol = jnp.concatenate([rcol] * (chunk // LANES), axis=1)

    def one_chunk(c, carry):
        first = (c * chunk).astype(F32)
        want_r = lax.broadcasted_iota(jnp.int32, (chunk, 1), 0).astype(F32) + first
        gather = jnp.where(rrow == want_r, 1.0, 0.0).astype(BF16)
        h = _dot(gather, h_ref[...]).astype(BF16)
        cm = _dot(gather, cg_hi) + _dot(gather, cg_lo)
        acts = []
        for e in range(n_exp):
            a = _dot(h, w1_ref[e])
            a = (a * jax.nn.sigmoid(a)) * _dot(h, w3_ref[e])
            acts.append((a * cm[:, e:e + 1]).astype(BF16))
        y = _dot(jnp.concatenate(acts, axis=-1), w2_ref[...]).astype(BF16)
        want_c = lax.broadcasted_iota(jnp.int32, (1, chunk), 1).astype(F32) + first
        scatter = jnp.where(rcol == want_c, 1.0, 0.0).astype(BF16)
        o_ref[...] += gate * _dot(scatter, y)
        return carry

    lax.fori_loop(0, (count[g] + chunk - 1) // chunk, one_chunk, 0)


def _moe_tile(s):
    for tile, chunk in ((768, 256), (1024, 384), (512, 256), (256, 128)):
        if s % tile == 0:
            return tile, chunk
    raise ValueError(f"unsupported sequence length {s}")


def _moe(x_new, h2, cmb, grp, mod, lw, i, lat_rows):
    b, s, d = x_new.shape
    tile, chunk = _moe_tile(s)
    ge = EXPERTS_PER_GROUP
    tpos = jnp.arange(tile)
    before = (tpos[None, :] < tpos[:, None]).astype(BF16)
    tok = lambda bb, j, e: (bb, j, 0)
    kern = functools.partial(_moe_kernel, chunk=chunk, lat_rows=lat_rows)
    return pl.pallas_call(
        kern,
        grid=(b, s // tile, N_EXPERT_GROUPS),
        in_specs=[
            pl.BlockSpec((None, tile, d), tok),
            pl.BlockSpec((None, tile, d), tok),
            pl.BlockSpec((None, tile, LANES), tok),
            pl.BlockSpec((None, 2 * N_EXPERT_GROUPS, tile), lambda bb, j, e: (bb, 0, j)),
            pl.BlockSpec((None, None, 2, N_MOD, d), lambda bb, j, e: (i, bb, 0, 0, 0)),
            _const_spec((tile, tile), lambda bb, j, e: (0, 0)),
            _const_spec((tile, tile), lambda bb, j, e: (0, 0)),
            pl.BlockSpec((None, ge, d, EXPERT_FF), lambda bb, j, e: (i, e, 0, 0)),
            pl.BlockSpec((None, ge, d, EXPERT_FF), lambda bb, j, e: (i, e, 0, 0)),
            pl.BlockSpec((None, ge * EXPERT_FF, d), lambda bb, j, e: (i, e, 0)),
        ],
        out_specs=pl.BlockSpec((None, tile, d), tok),
        out_shape=jax.ShapeDtypeStruct((b, s, d), F32),
        input_output_aliases={0: 0},
        scratch_shapes=[
            pltpu.VMEM((N_EXPERT_GROUPS, tile, LANES), F32),
            pltpu.VMEM((N_EXPERT_GROUPS, 8, tile), F32),
            pltpu.VMEM((N_EXPERT_GROUPS, tile, ge), F32),
            pltpu.SMEM((N_EXPERT_GROUPS,), jnp.int32),
        ],
        compiler_params=pltpu.CompilerParams(
            dimension_semantics=("arbitrary", "arbitrary", "arbitrary"), vmem_limit_bytes=VMEM_LIMIT),
    )(x_new, h2, cmb, grp, mod, before, before.T, lw["w1"], lw["w3"], lw["w2"])


def _prepare(p):
    dpt = p["w_in"].shape[0]
    w_in = p["w_in"].astype(BF16)
    zeros = lambda n: jnp.zeros((dpt, D_MODEL, n), BF16)
    kr_mid = KR_OFF + ROPE_HALF
    w_a = jnp.concatenate([
        w_in[..., KV_OFF:KR_OFF],
        zeros(NOPE_DIM), w_in[..., KR_OFF:Q_OFF], zeros(LANES - QK_DIM),
        zeros(NOPE_DIM), w_in[..., kr_mid:Q_OFF], w_in[..., KR_OFF:kr_mid], zeros(LANES - QK_DIM),
        w_in[..., Q_OFF:CONV_OFF]], axis=-1)
    gb_lo, gc_lo, u_lo = CONV_OFF, CONV_OFF + BRANCH_W, CONV_OFF + 2 * BRANCH_W
    w_h = jnp.concatenate([w_in[..., gc_lo:u_lo], w_in[..., u_lo:POOL_OFF],
                           w_in[..., POOL_OFF:SGU_OFF]], axis=-1)
    w_m = jnp.concatenate([w_in[..., gb_lo:gc_lo], w_in[..., SGU_OFF:]], axis=-1)
    w_ukv = p["w_ukv"].astype(BF16)
    w_uk = jnp.concatenate([w_ukv[..., :NOPE_DIM], jnp.zeros_like(w_ukv[..., :NOPE_DIM])], axis=-1)
    w_v = w_ukv[..., NOPE_DIM:]
    even = (jnp.arange(N_HEADS) % 2 == 0)[None, None, :, None]
    zero_v = jnp.zeros_like(w_v)
    w_uv = jnp.concatenate([jnp.where(even, w_v, zero_v), jnp.where(even, zero_v, w_v)], axis=-1)
    w_q = p["w_uq"].astype(BF16)
    w_uq = jnp.pad(w_q, ((0, 0), (0, 0), (0, 0), (0, LANES - QK_DIM)))
    rope_mid = NOPE_DIM + ROPE_HALF
    w_uqs = jnp.concatenate([
        jnp.zeros_like(w_q[..., :NOPE_DIM]), w_q[..., rope_mid:], w_q[..., NOPE_DIM:rope_mid],
        jnp.zeros_like(w_q[..., :LANES - QK_DIM])], axis=-1)
    eye = jnp.eye(SGU_TILE // SGU_CHUNK, dtype=F32)
    sgu_bd = jnp.einsum("ab,lgpq->lgapbq", eye, p["sgu_w"]).reshape(dpt, SGU_G, SGU_TILE, SGU_TILE)
    sgu_b = jnp.repeat(jnp.swapaxes(p["sgu_b"], 1, 2), SGU_CHUNK, axis=2)
    sgu_b = jnp.tile(sgu_b, (1, SGU_TILE // SGU_CHUNK, 1))
    return {
        "g1": p["norm1_g"][:, None, :], "g2": p["norm2_g"][:, None, :],
        "w_a": w_a, "w_h": w_h, "w_m": w_m,
        "g_kv": p["kv_a_norm_g"][:, None, :], "g_qa": p["q_a_norm_g"][:, None, :],
        "w_uk": w_uk.reshape(dpt, KV_LORA, N_HEADS * LANES),
        "w_uv": w_uv.reshape(dpt, KV_LORA, N_HEADS * LANES),
        "w_uq": w_uq.reshape(dpt, Q_LORA, N_HEADS * LANES),
        "w_uqs": w_uqs.reshape(dpt, Q_LORA, N_HEADS * LANES),
        "conv_w": p["conv_w"], "pool_w": p["pool_w"].astype(BF16),
        "pool_scale": p["pool_scale"][:, None, :], "g_sgu": p["sgu_norm_g"][:, None, :],
        "sgu_w": sgu_bd.astype(BF16), "sgu_b": sgu_b,
        "w_branch": p["w_branch"].astype(BF16), "w_o": p["w_o"].astype(BF16),
        "w1": p["moe_w1"].astype(BF16), "w3": p["moe_w3"].astype(BF16),
        "w2": p["moe_w2"].reshape(dpt, N_EXPERTS * EXPERT_FF, D_MODEL).astype(BF16),
    }


def _rope_tables(ctx_len, length, k_gain, q_gain):
    rows = length // GRID_W
    t = jnp.arange(rows * GRID_W)
    grid = jnp.stack([t // GRID_W, t % GRID_W], -1).astype(F32)
    inv_freq = ROPE_THETA ** (-jnp.arange(AXIS_PAIRS, dtype=F32) / AXIS_PAIRS)
    ang = jnp.concatenate([grid[:, :1] * inv_freq, grid[:, 1:] * inv_freq], -1)
    cos, sin = jnp.cos(ang), jnp.sin(ang)
    one = jnp.ones((length, NOPE_DIM), F32)
    tail = jnp.ones((length, LANES - QK_DIM), F32)
    zn = jnp.zeros((length, NOPE_DIM), F32)
    zt = jnp.zeros((length, LANES - QK_DIM), F32)
    rc = jnp.concatenate([one, cos, cos, tail], -1)
    rs = jnp.concatenate([zn, -sin, sin, zt], -1)
    ctx_id = lambda fill: jnp.full((ctx_len, LANES), fill, F32)
    rc = jnp.concatenate([rc, ctx_id(1.0)], 0)
    rs = jnp.concatenate([rs, ctx_id(0.0)], 0)

    def tables(gain, const):
        g = jnp.pad(gain, ((0, 0), (0, LANES - QK_DIM)))
        mid = NOPE_DIM + ROPE_HALF
        g_swap = jnp.concatenate([g[:, :NOPE_DIM], g[:, mid:QK_DIM], g[:, NOPE_DIM:mid],
                                  g[:, QK_DIM:]], -1)
        return (g[:, None, :] * rc[None]) * const, (g_swap[:, None, :] * rs[None]) * const

    gak, gbk = tables(k_gain, 1.0)
    gaq, gbq = tables(q_gain, ATTN_SCALE * LOG2E)
    return gak, gbk, gaq, gbq


def kernel(x, c, ctx, c_ctx, mod_w, mod_b, norm1_g, norm2_g, w_in, kv_a_norm_g, w_ukv, k_norm_g,
           q_a_norm_g, w_uq, q_norm_g, conv_w, pool_w, pool_scale, sgu_norm_g, sgu_w, sgu_b,
           w_branch, w_o, router_w, router_b, moe_w1, moe_w3, moe_w2):
    b, length, d = x.shape
    ctx_len = ctx.shape[1]
    assert d == D_MODEL and length % TOKEN_TILE == 0 and ctx_len % TOKEN_TILE == 0
    assert length % GRID_W == 0
    s = length + ctx_len

    lw = _prepare(dict(
        norm1_g=norm1_g, norm2_g=norm2_g, w_in=w_in, kv_a_norm_g=kv_a_norm_g, w_ukv=w_ukv,
        k_norm_g=k_norm_g, q_a_norm_g=q_a_norm_g, w_uq=w_uq, q_norm_g=q_norm_g, conv_w=conv_w,
        pool_w=pool_w, pool_scale=pool_scale, sgu_norm_g=sgu_norm_g, sgu_w=sgu_w, sgu_b=sgu_b,
        w_branch=w_branch, w_o=w_o, moe_w1=moe_w1, moe_w3=moe_w3, moe_w2=moe_w2))
    rope_tabs = _rope_tables(ctx_len, length, k_norm_g, q_norm_g)
    rw = router_w.T
    rb = router_b[:, None]

    rows = -(-(b + 1) // 8) * 8
    cvec = jnp.concatenate([c, c_ctx[None, :], jnp.zeros((rows - b - 1, d), F32)], 0)
    mod = _modulation(cvec, mod_w, mod_b).reshape(DEPTH, rows, N_MOD, d)
    mod = jnp.stack([jnp.broadcast_to(mod[:, b:b + 1], (DEPTH, b, N_MOD, d)), mod[:, :b]], axis=2)

    xs = jnp.concatenate([x, ctx], axis=1)
    for i in range(DEPTH):
        last = i == DEPTH - 1
        q, k, v = _qkv(xs, mod, lw, i, rope_tabs, length)
        attn = _attention(q, k, v, 0, length, 0, None)
        if not last:
            attn = _attention(q, k, v, length, ctx_len, length, attn)
        out_len = length if last else s
        outs = _mix(xs, mod, attn, lw, i, 0, length, 1, out_len, None, rw, rb)
        if not last:
            outs = _mix(xs, mod, attn, lw, i, length, ctx_len, 0, out_len, outs, rw, rb)
        x_new, h2, cmb, grp = outs
        xs = _moe(x_new, h2, cmb, grp, mod, lw, i, None if last else length)
    return xs
```

```python
import functools

import jax
import jax.numpy as jnp
from jax import lax
from jax.experimental import pallas as pl
from jax.experimental.pallas import tpu as pltpu

F32 = jnp.float32
BF16 = jnp.bfloat16

D_MODEL = 1024
DEPTH = 4
GRID_W = 64
EPS = 1e-6
N_MOD = 6
N_BRANCH = 4
BRANCH_W = 512
N_HEADS = 8
Q_LORA = 256
KV_LORA = 128
NOPE_DIM = 64
ROPE_DIM = 32
V_DIM = 64
QK_DIM = NOPE_DIM + ROPE_DIM
ROPE_HALF = ROPE_DIM // 2
AXIS_PAIRS = ROPE_DIM // 4
ROPE_THETA = 10000.0
ATTN_SCALE = QK_DIM ** -0.5
LOG2E = 1.4426950408889634
CONV_K = 3
POOL_WINDOWS = (2, 4, 8, 16)
POOL_GC = 128
SGU_CHUNK = 128
SGU_G = 4
N_EXPERTS = 16
N_EXPERT_GROUPS = 4
EXPERTS_PER_GROUP = 4
EXPERT_FF = 256

KV_OFF = 0
KR_OFF = KV_OFF + KV_LORA
Q_OFF = KR_OFF + ROPE_DIM
CONV_OFF = Q_OFF + Q_LORA
POOL_OFF = CONV_OFF + 3 * BRANCH_W
SGU_OFF = POOL_OFF + BRANCH_W
GATE_OFF = SGU_OFF + 2 * BRANCH_W

LANES = 128
HALO = 16
TOKEN_TILE = 256
SGU_TILE = 256
MIX_TILES = (512, 256)
ATTN_TILES = (512, 256)
VMEM_LIMIT = 56 * 1024 * 1024


def _rms(x, width=None):
    width = x.shape[-1] if width is None else width
    return x * lax.rsqrt(jnp.sum(x * x, axis=-1, keepdims=True) * (1.0 / width) + EPS)


def _norm_mod(x, g, shift, scale):
    return _rms(x) * (g * (1.0 + scale)) + shift


def _dot(a, b):
    return jnp.dot(a, b, preferred_element_type=F32)


def _const_spec(shape, index_map):
    return pl.BlockSpec(shape, index_map, pipeline_mode=pl.Buffered(1))


def _mod_kernel(c_ref, w_ref, b_ref, o_ref):
    c = c_ref[...]
    sc = c * jax.nn.sigmoid(c)
    o_ref[...] = jnp.dot(sc, w_ref[...], preferred_element_type=F32,
                         precision=lax.Precision.HIGHEST) + b_ref[...]


def _modulation(cvec, mod_w, mod_b):
    rows = cvec.shape[0]
    n = mod_w.shape[-1]
    bn = 1536
    return pl.pallas_call(
        _mod_kernel,
        grid=(DEPTH, n // bn),
        in_specs=[
            pl.BlockSpec((rows, D_MODEL), lambda i, j: (0, 0)),
            pl.BlockSpec((None, D_MODEL, bn), lambda i, j: (i, 0, j)),
            pl.BlockSpec((None, 1, bn), lambda i, j: (i, 0, j)),
        ],
        out_specs=pl.BlockSpec((None, rows, bn), lambda i, j: (i, 0, j)),
        out_shape=jax.ShapeDtypeStruct((DEPTH, rows, n), F32),
        compiler_params=pltpu.CompilerParams(
            dimension_semantics=("arbitrary", "arbitrary"), vmem_limit_bytes=VMEM_LIMIT),
    )(cvec, mod_w, mod_b.reshape(DEPTH, 1, n))


def _qkv_kernel(x_ref, mod_ref, g1_ref, wa_ref, gkv_ref, gq_ref, wuk_ref, wuv_ref, wuq_ref, wuqs_ref,
                gak_ref, gbk_ref, gaq_ref, gbq_ref, q_ref, k_ref, v_ref):
    x = x_ref[...]
    m = mod_ref[...]
    h = _norm_mod(x, g1_ref[...], m[0:1], m[1:2]).astype(BF16)
    p = _dot(h, wa_ref[...])
    ckv = (_rms(p[:, :KV_LORA]) * gkv_ref[...]).astype(BF16)
    kk = _dot(ckv, wuk_ref[...])
    vv = _dot(ckv, wuv_ref[...])
    kr = p[:, LANES:2 * LANES]
    krs = p[:, 2 * LANES:3 * LANES]
    cq = (_rms(p[:, 3 * LANES:]) * gq_ref[...]).astype(BF16)
    qq = _dot(cq, wuq_ref[...])
    qs = _dot(cq, wuqs_ref[...])
    gak, gbk, gaq, gbq = gak_ref[...], gbk_ref[...], gaq_ref[...], gbq_ref[...]
    k_shared = kr * gak + krs * gbk
    kr_sq = jnp.sum(kr * kr, axis=-1, keepdims=True)
    lane = lax.broadcasted_iota(jnp.int32, (1, LANES), 1)
    one_col = [jnp.where(lane == V_DIM, 1.0, 0.0), jnp.where(lane == 0, 1.0, 0.0)]

    for hd in range(N_HEADS):
        cols = slice(hd * LANES, (hd + 1) * LANES)
        kn = kk[:, cols]
        rk = lax.rsqrt((jnp.sum(kn * kn, axis=-1, keepdims=True) + kr_sq) * (1.0 / QK_DIM) + EPS)
        k_ref[hd] = ((kn * gak + k_shared) * rk).astype(BF16)
        qh = qq[:, cols]
        rq = lax.rsqrt(jnp.sum(qh * qh, axis=-1, keepdims=True) * (1.0 / QK_DIM) + EPS)
        q_ref[hd] = ((qh * gaq + qs[:, cols] * gbq) * rq).astype(BF16)
        v_ref[hd] = (vv[:, cols] + one_col[hd % 2]).astype(BF16)


def _qkv(xs, mod, lw, i, rope_tabs, lat_len):
    b, s, d = xs.shape
    tm = TOKEN_TILE
    n_lat_tiles = lat_len // tm

    def cw(shape):
        return _const_spec((None,) + shape, lambda bb, j: (i,) + (0,) * len(shape))

    head = lambda bb, j: (bb, 0, j, 0)
    rspec = pl.BlockSpec((None, tm, LANES), lambda bb, j: (i, j, 0))
    hspec = pl.BlockSpec((None, N_HEADS, tm, LANES), head)
    hshape = jax.ShapeDtypeStruct((b, N_HEADS, s, LANES), BF16)
    return pl.pallas_call(
        _qkv_kernel,
        grid=(b, s // tm),
        in_specs=[
            pl.BlockSpec((None, tm, d), lambda bb, j: (bb, j, 0)),
            pl.BlockSpec((None, None, None, N_MOD, d),
                         lambda bb, j: (i, bb, jnp.where(j < n_lat_tiles, 1, 0), 0, 0)),
            cw((1, d)), cw((d, 5 * LANES)), cw((1, KV_LORA)), cw((1, Q_LORA)),
            cw((KV_LORA, N_HEADS * LANES)), cw((KV_LORA, N_HEADS * LANES)),
            cw((Q_LORA, N_HEADS * LANES)), cw((Q_LORA, N_HEADS * LANES)),
            rspec, rspec, rspec, rspec,
        ],
        out_specs=[hspec, hspec, hspec],
        out_shape=[hshape, hshape, hshape],
        compiler_params=pltpu.CompilerParams(
            dimension_semantics=("arbitrary", "arbitrary"), vmem_limit_bytes=VMEM_LIMIT),
    )(xs, mod, lw["g1"], lw["w_a"], lw["g_kv"], lw["g_qa"], lw["w_uk"], lw["w_uv"], lw["w_uq"],
      lw["w_uqs"], *rope_tabs)


def _attn_kernel(*refs, key_lo):
    q_ref, k_ref, v_ref = refs[:3]
    o_ref = refs[-1]
    n_keys = k_ref.shape[1]
    lane = lax.broadcasted_iota(jnp.int32, (1, LANES), 1)
    for pair in range(N_HEADS // 2):
        outs = []
        for hd in (2 * pair, 2 * pair + 1):
            s = lax.dot_general(q_ref[hd], k_ref[hd, key_lo:n_keys, :], (((1,), (1,)), ((), ())),
                                preferred_element_type=F32)
            e = jnp.exp2(s - jnp.max(s, axis=-1, keepdims=True))
            o = _dot(e.astype(BF16), v_ref[hd, key_lo:n_keys, :])
            den = o[:, V_DIM:V_DIM + 1] if hd % 2 == 0 else o[:, 0:1]
            outs.append(o / den)
        o_ref[:, pair * LANES:(pair + 1) * LANES] = jnp.where(
            lane < V_DIM, outs[0], outs[1]).astype(BF16)


def _attention(q, k, v, q_start, q_len, key_lo, out_len, carry):
    b, _, s, _ = q.shape
    tq = next(t for t in ATTN_TILES if q_len % t == 0 and q_start % t == 0)
    base = q_start // tq
    in_specs = [
        pl.BlockSpec((None, N_HEADS, tq, LANES), lambda bb, j: (bb, 0, base + j, 0)),
        pl.BlockSpec((None, N_HEADS, s, LANES), lambda bb, j: (bb, 0, 0, 0)),
        pl.BlockSpec((None, N_HEADS, s, LANES), lambda bb, j: (bb, 0, 0, 0)),
    ]
    args = [q, k, v]
    aliases = {}
    if carry is not None:
        in_specs.append(pl.BlockSpec(memory_space=pl.ANY))
        args.append(carry)
        aliases = {3: 0}
    return pl.pallas_call(
        functools.partial(_attn_kernel, key_lo=key_lo),
        grid=(b, q_len // tq),
        in_specs=in_specs,
        out_specs=pl.BlockSpec((None, tq, BRANCH_W), lambda bb, j: (bb, base + j, 0)),
        out_shape=jax.ShapeDtypeStruct((b, out_len, BRANCH_W), BF16),
        input_output_aliases=aliases,
        compiler_params=pltpu.CompilerParams(
            dimension_semantics=("arbitrary", "arbitrary"), vmem_limit_bytes=VMEM_LIMIT),
    )(*args)


def _route(logits, rbias):
    row = lax.broadcasted_iota(jnp.int32, logits.shape, 0)
    slot = row % EXPERTS_PER_GROUP
    s = jax.nn.sigmoid(logits)
    sb = s + rbias

    def mate(x, dd):
        fwd = pltpu.roll(x, N_EXPERTS - dd, 0)
        back = pltpu.roll(x, EXPERTS_PER_GROUP - dd, 0)
        return jnp.where(slot + dd < EXPERTS_PER_GROUP, fwd, back)

    mates = [mate(sb, dd) for dd in (1, 2, 3)]
    vals = [sb] + mates
    gscore = None
    for a in range(4):
        for c in range(a + 1, 4):
            pair = vals[a] + vals[c]
            gscore = pair if gscore is None else jnp.maximum(gscore, pair)
    group_ok = None
    for gg in (1, 2, 3):
        other = pltpu.roll(gscore, EXPERTS_PER_GROUP * gg, 0)
        other_is_lower = row >= EXPERTS_PER_GROUP * gg
        wins = jnp.logical_or(gscore > other,
                              jnp.logical_and(gscore == other, jnp.logical_not(other_is_lower)))
        group_ok = wins if group_ok is None else jnp.logical_and(group_ok, wins)
    rank = jnp.zeros(logits.shape, jnp.int32)
    for dd, mv in zip((1, 2, 3), mates):
        wrapped = slot + dd >= EXPERTS_PER_GROUP
        beats = jnp.logical_or(mv > sb, jnp.logical_and(wrapped, mv == sb))
        rank = rank + beats.astype(jnp.int32)
    chosen = jnp.logical_and(rank < 2, group_ok)
    w = jnp.where(chosen, s, 0.0)
    group_f = jnp.where(group_ok, 1.0, 0.0)
    group_onehot = jnp.concatenate(
        [group_f[gg * EXPERTS_PER_GROUP:gg * EXPERTS_PER_GROUP + 1]
         for gg in range(N_EXPERT_GROUPS)], axis=0)
    return w / jnp.sum(w, axis=0, keepdims=True), group_onehot


N_MIX_INPUTS = 19


def _mix_kernel(*refs, seg_tiles):
    (x_ref, xp_ref, xn_ref, mod_ref, g1_ref, g2_ref, wh_ref, wm_ref, attn_ref,
     convw_ref, poolw_ref, pscale_ref, gsgu_ref, sguw_ref, sgub_ref, wbr_ref, wo_ref,
     rw_ref, rb_ref) = refs[:N_MIX_INPUTS]
    xo_ref, h2_ref, cmb_ref, grp_ref, hcat_ref = refs[-5:]
    tm = x_ref.shape[0]
    j = pl.program_id(1)
    m = mod_ref[...]
    g1 = g1_ref[...]
    x = x_ref[...]

    def hnorm(xv):
        return _norm_mod(xv, g1, m[0:1], m[1:2]).astype(BF16)

    hm = hnorm(x)
    hcat_ref[0:HALO, :] = hnorm(xp_ref[...])
    hcat_ref[HALO:HALO + tm, :] = hm
    hcat_ref[HALO + tm:, :] = hnorm(xn_ref[...])

    nrow = tm + 2 * HALO
    row = lax.broadcasted_iota(jnp.int32, (nrow, 1), 0)
    keep = jnp.logical_and(jnp.logical_or(row >= HALO, j > 0),
                           jnp.logical_or(row < HALO + tm, j < seg_tiles - 1))
    ph = _dot(hcat_ref[...], wh_ref[...])
    ph = jnp.where(keep, ph, 0.0)

    def shifted(a, k):
        return pltpu.roll(a, (-k) % nrow, 0)[HALO:HALO + tm]

    cu = ph[:, :BRANCH_W] * ph[:, BRANCH_W:2 * BRANCH_W]
    cw = convw_ref[...]
    conv = shifted(cu, -1) * cw[0:1] + cu[HALO:HALO + tm] * cw[1:2] + shifted(cu, 1) * cw[2:3]
    gb = _dot(hm, wm_ref[:, 0:BRANCH_W])
    br_conv = gb * conv

    pp = ph[:, 2 * BRANCH_W:]
    seg_len = seg_tiles * tm
    pos = j * tm + lax.broadcasted_iota(jnp.int32, (tm, 1), 0)
    level = pp + pltpu.roll(pp, 1, 0)
    pooled = []
    for gi, win in enumerate(POOL_WINDOWS):
        if gi > 0:
            half = win // 4
            level = level[:, POOL_GC:]
            level = pltpu.roll(level, nrow - half, 0) + pltpu.roll(level, half, 0)
        cnt = (jnp.minimum(pos + win // 2, seg_len) - jnp.maximum(pos - win // 2, 0)).astype(F32)
        own = pp[HALO:HALO + tm, gi * POOL_GC:(gi + 1) * POOL_GC]
        pooled.append(level[HALO:HALO + tm, :POOL_GC] / cnt - own)
    br_pool = jnp.concatenate(
        [_dot(pooled[gi].astype(BF16), poolw_ref[gi]) for gi in range(len(POOL_WINDOWS))], axis=-1)
    br_pool = br_pool * pscale_ref[...]

    z = jax.nn.gelu(_dot(hm, wm_ref[:, BRANCH_W:3 * BRANCH_W]), approximate=True)
    zu = z[:, :BRANCH_W]
    zv = (_rms(z[:, BRANCH_W:]) * gsgu_ref[...]).astype(BF16)
    mixed = jnp.concatenate([
        jnp.concatenate([_dot(sguw_ref[gi], zv[r0:r0 + SGU_TILE, gi * LANES:(gi + 1) * LANES])
                         for gi in range(SGU_G)], axis=-1) + sgub_ref[...]
        for r0 in range(0, tm, SGU_TILE)], axis=0)
    br_sgu = zu * mixed

    branches = (attn_ref[...], br_conv.astype(BF16), br_pool.astype(BF16), br_sgu.astype(BF16))
    merged = None
    for nb, br in enumerate(branches):
        lo = 3 * BRANCH_W + nb * D_MODEL
        gate = jax.nn.sigmoid(_dot(hm, wm_ref[:, lo:lo + D_MODEL]))
        term = gate * _dot(br, wbr_ref[nb])
        merged = term if merged is None else merged + term
    x_new = x + m[2:3] * _dot(merged.astype(BF16), wo_ref[...])
    xo_ref[...] = x_new

    h2 = _norm_mod(x_new, g2_ref[...], m[3:4], m[4:5])
    h2_ref[...] = h2.astype(BF16)
    logits = lax.dot_general(rw_ref[...], h2, (((1,), (1,)), ((), ())),
                             preferred_element_type=F32, precision=lax.Precision.HIGHEST)
    cmb_t, grp_t = _route(logits, rb_ref[...])
    pad = jnp.zeros((LANES - N_EXPERTS - N_EXPERT_GROUPS, tm), F32)
    cmb_ref[...] = jnp.concatenate([cmb_t, grp_t, pad], axis=0).T
    grp_ref[...] = jnp.concatenate([grp_t, jnp.zeros_like(grp_t)], axis=0)


def _mix(xs, mod, attn, lw, i, seg_start, seg_len, seg_id, out_len, carry, router_w, router_b):
    b, s, d = xs.shape
    tm = next(t for t in MIX_TILES if seg_len % t == 0 and seg_start % t == 0)
    seg_tiles = seg_len // tm
    base = seg_start // tm
    hb = tm // HALO
    base_h = seg_start // HALO
    last_h = s // HALO - 1

    def cw(shape):
        return _const_spec((None,) + shape, lambda bb, j: (i,) + (0,) * len(shape))

    tok = lambda bb, j: (bb, base + j, 0)
    in_specs = [
        pl.BlockSpec((None, tm, d), tok),
        pl.BlockSpec((None, HALO, d), lambda bb, j: (bb, jnp.maximum(base_h + j * hb - 1, 0), 0)),
        pl.BlockSpec((None, HALO, d),
                     lambda bb, j: (bb, jnp.minimum(base_h + (j + 1) * hb, last_h), 0)),
        pl.BlockSpec((None, None, None, N_MOD, d), lambda bb, j: (i, bb, seg_id, 0, 0)),
        cw((1, d)), cw((1, d)),
        cw((d, 3 * BRANCH_W)), cw((d, 3 * BRANCH_W + N_BRANCH * D_MODEL)),
        pl.BlockSpec((None, tm, BRANCH_W), tok),
        cw((CONV_K, BRANCH_W)), cw((len(POOL_WINDOWS), POOL_GC, POOL_GC)), cw((1, BRANCH_W)),
        cw((1, BRANCH_W)), cw((SGU_G, SGU_TILE, SGU_TILE)), cw((SGU_TILE, BRANCH_W)),
        cw((N_BRANCH, BRANCH_W, D_MODEL)), cw((D_MODEL, D_MODEL)),
        _const_spec((N_EXPERTS, d), lambda bb, j: (0, 0)),
        _const_spec((N_EXPERTS, 1), lambda bb, j: (0, 0)),
    ]
    assert len(in_specs) == N_MIX_INPUTS
    args = [xs, xs, xs, mod, lw["g1"], lw["g2"], lw["w_h"], lw["w_m"], attn,
            lw["conv_w"], lw["pool_w"], lw["pool_scale"], lw["g_sgu"], lw["sgu_w"], lw["sgu_b"],
            lw["w_branch"], lw["w_o"], router_w, router_b]
    aliases = {}
    if carry is not None:
        in_specs += [pl.BlockSpec(memory_space=pl.ANY)] * len(carry)
        args += list(carry)
        aliases = {N_MIX_INPUTS + n: n for n in range(len(carry))}
    return pl.pallas_call(
        functools.partial(_mix_kernel, seg_tiles=seg_tiles),
        grid=(b, seg_tiles),
        in_specs=in_specs,
        out_specs=[
            pl.BlockSpec((None, tm, d), tok),
            pl.BlockSpec((None, tm, d), tok),
            pl.BlockSpec((None, tm, LANES), tok),
            pl.BlockSpec((None, 2 * N_EXPERT_GROUPS, tm), lambda bb, j: (bb, 0, base + j)),
        ],
        out_shape=[
            jax.ShapeDtypeStruct((b, out_len, d), F32),
            jax.ShapeDtypeStruct((b, out_len, d), BF16),
            jax.ShapeDtypeStruct((b, out_len, LANES), F32),
            jax.ShapeDtypeStruct((b, 2 * N_EXPERT_GROUPS, out_len), F32),
        ],
        input_output_aliases=aliases,
        scratch_shapes=[pltpu.VMEM((tm + 2 * HALO, d), BF16)],
        compiler_params=pltpu.CompilerParams(
            dimension_semantics=("arbitrary", "arbitrary"), vmem_limit_bytes=VMEM_LIMIT),
    )(*args)


def _moe_kernel(x_ref, h_ref, cmb_ref, grp_ref, mod_ref, before_ref, after_ref,
                w1_ref, w3_ref, w2_ref, o_ref, rank_col, rank_row, group_cmb, count, *, chunk, tail, lat_rows):
    tile = x_ref.shape[0]
    g = pl.program_id(2)
    n_exp = w1_ref.shape[0]

    @pl.when(g == 0)
    def _():
        o_ref[...] = x_ref[...]
        lane = lax.broadcasted_iota(jnp.int32, (1, LANES), 1)
        is_group = jnp.logical_and(lane >= N_EXPERTS, lane < N_EXPERTS + N_EXPERT_GROUPS)
        member_c = jnp.where(is_group, cmb_ref[...], 0.0)
        member_r = grp_ref[...]
        earlier_c = _dot(before_ref[...], member_c.astype(BF16))
        earlier_r = _dot(member_r.astype(BF16), after_ref[...])
        for gg in range(N_EXPERT_GROUPS):
            col = slice(N_EXPERTS + gg, N_EXPERTS + gg + 1)
            rank_col[gg] = jnp.broadcast_to(
                jnp.where(member_c[:, col] > 0.0, earlier_c[:, col], -1.0), (tile, LANES))
            rank_row[gg] = jnp.broadcast_to(
                jnp.where(member_r[gg:gg + 1] > 0.0, earlier_r[gg:gg + 1], -1.0), rank_row.shape[1:])
            group_cmb[gg] = cmb_ref[:, gg * EXPERTS_PER_GROUP:(gg + 1) * EXPERTS_PER_GROUP]
            count[gg] = jnp.sum(member_r[gg:gg + 1]).astype(jnp.int32)

    gate = mod_ref[1, N_MOD - 1:N_MOD]
    if lat_rows is not None:
        pos = pl.program_id(1) * tile + lax.broadcasted_iota(jnp.int32, (tile, 1), 0)
        gate = jnp.where(pos < lat_rows, gate, mod_ref[0, N_MOD - 1:N_MOD])
    cg = group_cmb[g]
    cg_hi = cg.astype(BF16)
    cg_lo = (cg - cg_hi.astype(F32)).astype(BF16)
    rrow = rank_row[g][0:1, :]
    rcol = rank_col[g]

    def run_chunk(first_rank, size):
        first = first_rank.astype(F32)
        want_r = lax.broadcasted_iota(jnp.int32, (size, 1), 0).astype(F32) + first
        gather = jnp.where(rrow == want_r, 1.0, 0.0).astype(BF16)
        h = _dot(gather, h_ref[...]).astype(BF16)
        cm = _dot(gather, cg_hi) + _dot(gather, cg_lo)
        acts = []
        for e in range(n_exp):
            a = _dot(h, w1_ref[e])
            a = (a * jax.nn.sigmoid(a)) * _dot(h, w3_ref[e])
            acts.append((a * cm[:, e:e + 1]).astype(BF16))
        y = _dot(jnp.concatenate(acts, axis=-1), w2_ref[...]).astype(BF16)
        want_c = lax.broadcasted_iota(jnp.int32, (1, size), 1).astype(F32) + first
        rank_lanes = jnp.concatenate([rcol] * (size // LANES), axis=1)
        scatter = jnp.where(rank_lanes == want_c, 1.0, 0.0).astype(BF16)
        o_ref[...] += gate * _dot(scatter, y)

    cnt = count[g]
    left = cnt % chunk
    n_full = cnt // chunk + jnp.where(left > tail, 1, 0)

    def full_chunk(c, carry):
        run_chunk(c * chunk, chunk)
        return carry

    lax.fori_loop(0, n_full, full_chunk, 0)

    @pl.when(jnp.logical_and(left > 0, left <= tail))
    def _():
        run_chunk(n_full * chunk, tail)


def _moe_tile(s):
    for tile, chunk in ((768, 256), (1024, 384), (512, 256), (256, 128)):
        if s % tile == 0:
            return tile, chunk
    raise ValueError(f"unsupported sequence length {s}")


def _moe(x_new, h2, cmb, grp, mod, lw, i, lat_rows):
    b, s, d = x_new.shape
    tile, chunk = _moe_tile(s)
    ge = EXPERTS_PER_GROUP
    tpos = jnp.arange(tile)
    before = (tpos[None, :] < tpos[:, None]).astype(BF16)
    tok = lambda bb, j, e: (bb, j, 0)
    kern = functools.partial(_moe_kernel, chunk=chunk, tail=LANES, lat_rows=lat_rows)
    return pl.pallas_call(
        kern,
        grid=(b, s // tile, N_EXPERT_GROUPS),
        in_specs=[
            pl.BlockSpec((None, tile, d), tok),
            pl.BlockSpec((None, tile, d), tok),
            pl.BlockSpec((None, tile, LANES), tok),
            pl.BlockSpec((None, 2 * N_EXPERT_GROUPS, tile), lambda bb, j, e: (bb, 0, j)),
            pl.BlockSpec((None, None, 2, N_MOD, d), lambda bb, j, e: (i, bb, 0, 0, 0)),
            _const_spec((tile, tile), lambda bb, j, e: (0, 0)),
            _const_spec((tile, tile), lambda bb, j, e: (0, 0)),
            pl.BlockSpec((None, ge, d, EXPERT_FF), lambda bb, j, e: (i, e, 0, 0)),
            pl.BlockSpec((None, ge, d, EXPERT_FF), lambda bb, j, e: (i, e, 0, 0)),
            pl.BlockSpec((None, ge * EXPERT_FF, d), lambda bb, j, e: (i, e, 0)),
        ],
        out_specs=pl.BlockSpec((None, tile, d), tok),
        out_shape=jax.ShapeDtypeStruct((b, s, d), F32),
        input_output_aliases={0: 0},
        scratch_shapes=[
            pltpu.VMEM((N_EXPERT_GROUPS, tile, LANES), F32),
            pltpu.VMEM((N_EXPERT_GROUPS, 8, tile), F32),
            pltpu.VMEM((N_EXPERT_GROUPS, tile, ge), F32),
            pltpu.SMEM((N_EXPERT_GROUPS,), jnp.int32),
        ],
        compiler_params=pltpu.CompilerParams(
            dimension_semantics=("arbitrary", "arbitrary", "arbitrary"), vmem_limit_bytes=VMEM_LIMIT),
    )(x_new, h2, cmb, grp, mod, before, before.T, lw["w1"], lw["w3"], lw["w2"])


def _prepare(p):
    dpt = p["w_in"].shape[0]
    w_in = p["w_in"].astype(BF16)
    zeros = lambda n: jnp.zeros((dpt, D_MODEL, n), BF16)
    kr_mid = KR_OFF + ROPE_HALF
    w_a = jnp.concatenate([
        w_in[..., KV_OFF:KR_OFF],
        zeros(NOPE_DIM), w_in[..., KR_OFF:Q_OFF], zeros(LANES - QK_DIM),
        zeros(NOPE_DIM), w_in[..., kr_mid:Q_OFF], w_in[..., KR_OFF:kr_mid], zeros(LANES - QK_DIM),
        w_in[..., Q_OFF:CONV_OFF]], axis=-1)
    gb_lo, gc_lo, u_lo = CONV_OFF, CONV_OFF + BRANCH_W, CONV_OFF + 2 * BRANCH_W
    w_h = jnp.concatenate([w_in[..., gc_lo:u_lo], w_in[..., u_lo:POOL_OFF],
                           w_in[..., POOL_OFF:SGU_OFF]], axis=-1)
    w_m = jnp.concatenate([w_in[..., gb_lo:gc_lo], w_in[..., SGU_OFF:]], axis=-1)
    w_ukv = p["w_ukv"].astype(BF16)
    w_uk = jnp.concatenate([w_ukv[..., :NOPE_DIM], jnp.zeros_like(w_ukv[..., :NOPE_DIM])], axis=-1)
    w_v = w_ukv[..., NOPE_DIM:]
    even = (jnp.arange(N_HEADS) % 2 == 0)[None, None, :, None]
    zero_v = jnp.zeros_like(w_v)
    w_uv = jnp.concatenate([jnp.where(even, w_v, zero_v), jnp.where(even, zero_v, w_v)], axis=-1)
    w_q = p["w_uq"].astype(BF16)
    w_uq = jnp.pad(w_q, ((0, 0), (0, 0), (0, 0), (0, LANES - QK_DIM)))
    rope_mid = NOPE_DIM + ROPE_HALF
    w_uqs = jnp.concatenate([
        jnp.zeros_like(w_q[..., :NOPE_DIM]), w_q[..., rope_mid:], w_q[..., NOPE_DIM:rope_mid],
        jnp.zeros_like(w_q[..., :LANES - QK_DIM])], axis=-1)
    eye = jnp.eye(SGU_TILE // SGU_CHUNK, dtype=F32)
    sgu_bd = jnp.einsum("ab,lgpq->lgapbq", eye, p["sgu_w"]).reshape(dpt, SGU_G, SGU_TILE, SGU_TILE)
    sgu_b = jnp.repeat(jnp.swapaxes(p["sgu_b"], 1, 2), SGU_CHUNK, axis=2)
    sgu_b = jnp.tile(sgu_b, (1, SGU_TILE // SGU_CHUNK, 1))
    return {
        "g1": p["norm1_g"][:, None, :], "g2": p["norm2_g"][:, None, :],
        "w_a": w_a, "w_h": w_h, "w_m": w_m,
        "g_kv": p["kv_a_norm_g"][:, None, :], "g_qa": p["q_a_norm_g"][:, None, :],
        "w_uk": w_uk.reshape(dpt, KV_LORA, N_HEADS * LANES),
        "w_uv": w_uv.reshape(dpt, KV_LORA, N_HEADS * LANES),
        "w_uq": w_uq.reshape(dpt, Q_LORA, N_HEADS * LANES),
        "w_uqs": w_uqs.reshape(dpt, Q_LORA, N_HEADS * LANES),
        "conv_w": p["conv_w"], "pool_w": p["pool_w"].astype(BF16),
        "pool_scale": p["pool_scale"][:, None, :], "g_sgu": p["sgu_norm_g"][:, None, :],
        "sgu_w": sgu_bd.astype(BF16), "sgu_b": sgu_b,
        "w_branch": p["w_branch"].astype(BF16), "w_o": p["w_o"].astype(BF16),
        "w1": p["moe_w1"].astype(BF16), "w3": p["moe_w3"].astype(BF16),
        "w2": p["moe_w2"].reshape(dpt, N_EXPERTS * EXPERT_FF, D_MODEL).astype(BF16),
    }


def _rope_tables(ctx_len, length, k_gain, q_gain):
    rows = length // GRID_W
    t = jnp.arange(rows * GRID_W)
    grid = jnp.stack([t // GRID_W, t % GRID_W], -1).astype(F32)
    inv_freq = ROPE_THETA ** (-jnp.arange(AXIS_PAIRS, dtype=F32) / AXIS_PAIRS)
    ang = jnp.concatenate([grid[:, :1] * inv_freq, grid[:, 1:] * inv_freq], -1)
    cos, sin = jnp.cos(ang), jnp.sin(ang)
    one = jnp.ones((length, NOPE_DIM), F32)
    tail = jnp.ones((length, LANES - QK_DIM), F32)
    zn = jnp.zeros((length, NOPE_DIM), F32)
    zt = jnp.zeros((length, LANES - QK_DIM), F32)
    rc = jnp.concatenate([one, cos, cos, tail], -1)
    rs = jnp.concatenate([zn, -sin, sin, zt], -1)
    ctx_id = lambda fill: jnp.full((ctx_len, LANES), fill, F32)
    rc = jnp.concatenate([rc, ctx_id(1.0)], 0)
    rs = jnp.concatenate([rs, ctx_id(0.0)], 0)

    def tables(gain, const):
        g = jnp.pad(gain, ((0, 0), (0, LANES - QK_DIM)))
        mid = NOPE_DIM + ROPE_HALF
        g_swap = jnp.concatenate([g[:, :NOPE_DIM], g[:, mid:QK_DIM], g[:, NOPE_DIM:mid],
                                  g[:, QK_DIM:]], -1)
        return (g[:, None, :] * rc[None]) * const, (g_swap[:, None, :] * rs[None]) * const

    gak, gbk = tables(k_gain, 1.0)
    gaq, gbq = tables(q_gain, ATTN_SCALE * LOG2E)
    return gak, gbk, gaq, gbq


def kernel(x, c, ctx, c_ctx, mod_w, mod_b, norm1_g, norm2_g, w_in, kv_a_norm_g, w_ukv, k_norm_g,
           q_a_norm_g, w_uq, q_norm_g, conv_w, pool_w, pool_scale, sgu_norm_g, sgu_w, sgu_b,
           w_branch, w_o, router_w, router_b, moe_w1, moe_w3, moe_w2):
    b, length, d = x.shape
    ctx_len = ctx.shape[1]
    assert d == D_MODEL and length % TOKEN_TILE == 0 and ctx_len % TOKEN_TILE == 0
    assert length % GRID_W == 0
    s = length + ctx_len

    lw = _prepare(dict(
        norm1_g=norm1_g, norm2_g=norm2_g, w_in=w_in, kv_a_norm_g=kv_a_norm_g, w_ukv=w_ukv,
        k_norm_g=k_norm_g, q_a_norm_g=q_a_norm_g, w_uq=w_uq, q_norm_g=q_norm_g, conv_w=conv_w,
        pool_w=pool_w, pool_scale=pool_scale, sgu_norm_g=sgu_norm_g, sgu_w=sgu_w, sgu_b=sgu_b,
        w_branch=w_branch, w_o=w_o, moe_w1=moe_w1, moe_w3=moe_w3, moe_w2=moe_w2))
    rope_tabs = _rope_tables(ctx_len, length, k_norm_g, q_norm_g)
    rw = router_w.T
    rb = router_b[:, None]

    rows = -(-(b + 1) // 8) * 8
    cvec = jnp.concatenate([c, c_ctx[None, :], jnp.zeros((rows - b - 1, d), F32)], 0)
    mod = _modulation(cvec, mod_w, mod_b).reshape(DEPTH, rows, N_MOD, d)
    mod = jnp.stack([jnp.broadcast_to(mod[:, b:b + 1], (DEPTH, b, N_MOD, d)), mod[:, :b]], axis=2)

    xs = jnp.concatenate([x, ctx], axis=1)
    for i in range(DEPTH):
        last = i == DEPTH - 1
        q, k, v = _qkv(xs, mod, lw, i, rope_tabs, length)
        out_len = length if last else s
        attn = _attention(q, k, v, 0, length, 0, out_len, None)
        if not last:
            attn = _attention(q, k, v, length, ctx_len, length, out_len, attn)
        outs = _mix(xs, mod, attn, lw, i, 0, length, 1, out_len, None, rw, rb)
        if not last:
            outs = _mix(xs, mod, attn, lw, i, length, ctx_len, 0, out_len, outs, rw, rb)
        x_new, h2, cmb, grp = outs
        xs = _moe(x_new, h2, cmb, grp, mod, lw, i, None if last else length)
    return xs
```

```python
import functools

import jax
import jax.numpy as jnp
from jax import lax
from jax.experimental import pallas as pl
from jax.experimental.pallas import tpu as pltpu

F32 = jnp.float32
BF16 = jnp.bfloat16

D_MODEL = 1024
DEPTH = 4
GRID_W = 64
EPS = 1e-6
N_MOD = 6
N_BRANCH = 4
BRANCH_W = 512
N_HEADS = 8
Q_LORA = 256
KV_LORA = 128
NOPE_DIM = 64
ROPE_DIM = 32
V_DIM = 64
QK_DIM = NOPE_DIM + ROPE_DIM
ROPE_HALF = ROPE_DIM // 2
AXIS_PAIRS = ROPE_DIM // 4
ROPE_THETA = 10000.0
ATTN_SCALE = QK_DIM ** -0.5
LOG2E = 1.4426950408889634
CONV_K = 3
POOL_WINDOWS = (2, 4, 8, 16)
POOL_GC = 128
SGU_CHUNK = 128
SGU_G = 4
N_EXPERTS = 16
N_EXPERT_GROUPS = 4
EXPERTS_PER_GROUP = 4
EXPERT_FF = 256

KV_OFF = 0
KR_OFF = KV_OFF + KV_LORA
Q_OFF = KR_OFF + ROPE_DIM
CONV_OFF = Q_OFF + Q_LORA
POOL_OFF = CONV_OFF + 3 * BRANCH_W
SGU_OFF = POOL_OFF + BRANCH_W
GATE_OFF = SGU_OFF + 2 * BRANCH_W

LANES = 128
HALO = 16
TOKEN_TILE = 256
SGU_TILE = 256
MIX_TILES = (512, 256)
ATTN_TILES = (512, 256)
VMEM_LIMIT = 56 * 1024 * 1024


def _rms(x, width=None):
    width = x.shape[-1] if width is None else width
    return x * lax.rsqrt(jnp.sum(x * x, axis=-1, keepdims=True) * (1.0 / width) + EPS)


def _norm_mod(x, g, shift, scale):
    return _rms(x) * (g * (1.0 + scale)) + shift


def _dot(a, b):
    return jnp.dot(a, b, preferred_element_type=F32)


def _const_spec(shape, index_map):
    return pl.BlockSpec(shape, index_map, pipeline_mode=pl.Buffered(1))


def _mod_kernel(c_ref, w_ref, b_ref, o_ref):
    c = c_ref[...]
    sc = c * jax.nn.sigmoid(c)
    o_ref[...] = jnp.dot(sc, w_ref[...], preferred_element_type=F32,
                         precision=lax.Precision.HIGHEST) + b_ref[...]


def _modulation(cvec, mod_w, mod_b):
    rows = cvec.shape[0]
    n = mod_w.shape[-1]
    bn = 1536
    return pl.pallas_call(
        _mod_kernel,
        grid=(DEPTH, n // bn),
        in_specs=[
            pl.BlockSpec((rows, D_MODEL), lambda i, j: (0, 0)),
            pl.BlockSpec((None, D_MODEL, bn), lambda i, j: (i, 0, j)),
            pl.BlockSpec((None, 1, bn), lambda i, j: (i, 0, j)),
        ],
        out_specs=pl.BlockSpec((None, rows, bn), lambda i, j: (i, 0, j)),
        out_shape=jax.ShapeDtypeStruct((DEPTH, rows, n), F32),
        compiler_params=pltpu.CompilerParams(
            dimension_semantics=("arbitrary", "arbitrary"), vmem_limit_bytes=VMEM_LIMIT),
    )(cvec, mod_w, mod_b.reshape(DEPTH, 1, n))


def _qkv_kernel(x_ref, mod_ref, g1_ref, wa_ref, gkv_ref, gq_ref, wuk_ref, wuv_ref, wuq_ref, wuqs_ref,
                gak_ref, gbk_ref, gaq_ref, gbq_ref, q_ref, k_ref, v_ref):
    x = x_ref[...]
    m = mod_ref[...]
    h = _norm_mod(x, g1_ref[...], m[0:1], m[1:2]).astype(BF16)
    p = _dot(h, wa_ref[...])
    ckv = (_rms(p[:, :KV_LORA]) * gkv_ref[...]).astype(BF16)
    kk = _dot(ckv, wuk_ref[...])
    vv = _dot(ckv, wuv_ref[...])
    kr = p[:, LANES:2 * LANES]
    krs = p[:, 2 * LANES:3 * LANES]
    cq = (_rms(p[:, 3 * LANES:]) * gq_ref[...]).astype(BF16)
    qq = _dot(cq, wuq_ref[...])
    qs = _dot(cq, wuqs_ref[...])
    gak, gbk, gaq, gbq = gak_ref[...], gbk_ref[...], gaq_ref[...], gbq_ref[...]
    k_shared = kr * gak + krs * gbk
    kr_sq = jnp.sum(kr * kr, axis=-1, keepdims=True)
    lane = lax.broadcasted_iota(jnp.int32, (1, LANES), 1)
    one_col = [jnp.where(lane == V_DIM, 1.0, 0.0), jnp.where(lane == 0, 1.0, 0.0)]

    for hd in range(N_HEADS):
        cols = slice(hd * LANES, (hd + 1) * LANES)
        kn = kk[:, cols]
        rk = lax.rsqrt((jnp.sum(kn * kn, axis=-1, keepdims=True) + kr_sq) * (1.0 / QK_DIM) + EPS)
        k_ref[hd] = ((kn * gak + k_shared) * rk).astype(BF16)
        qh = qq[:, cols]
        rq = lax.rsqrt(jnp.sum(qh * qh, axis=-1, keepdims=True) * (1.0 / QK_DIM) + EPS)
        q_ref[hd] = ((qh * gaq + qs[:, cols] * gbq) * rq).astype(BF16)
        v_ref[hd] = (vv[:, cols] + one_col[hd % 2]).astype(BF16)


def _qkv(xs, mod, lw, i, rope_tabs, lat_len):
    b, s, d = xs.shape
    tm = TOKEN_TILE
    n_lat_tiles = lat_len // tm

    def cw(shape):
        return _const_spec((None,) + shape, lambda bb, j: (i,) + (0,) * len(shape))

    head = lambda bb, j: (bb, 0, j, 0)
    rspec = pl.BlockSpec((None, tm, LANES), lambda bb, j: (i, j, 0))
    hspec = pl.BlockSpec((None, N_HEADS, tm, LANES), head)
    hshape = jax.ShapeDtypeStruct((b, N_HEADS, s, LANES), BF16)
    return pl.pallas_call(
        _qkv_kernel,
        grid=(b, s // tm),
        in_specs=[
            pl.BlockSpec((None, tm, d), lambda bb, j: (bb, j, 0)),
            pl.BlockSpec((None, None, None, N_MOD, d),
                         lambda bb, j: (i, bb, jnp.where(j < n_lat_tiles, 1, 0), 0, 0)),
            cw((1, d)), cw((d, 5 * LANES)), cw((1, KV_LORA)), cw((1, Q_LORA)),
            cw((KV_LORA, N_HEADS * LANES)), cw((KV_LORA, N_HEADS * LANES)),
            cw((Q_LORA, N_HEADS * LANES)), cw((Q_LORA, N_HEADS * LANES)),
            rspec, rspec, rspec, rspec,
        ],
        out_specs=[hspec, hspec, hspec],
        out_shape=[hshape, hshape, hshape],
        compiler_params=pltpu.CompilerParams(
            dimension_semantics=("arbitrary", "arbitrary"), vmem_limit_bytes=VMEM_LIMIT),
    )(xs, mod, lw["g1"], lw["w_a"], lw["g_kv"], lw["g_qa"], lw["w_uk"], lw["w_uv"], lw["w_uq"],
      lw["w_uqs"], *rope_tabs)


def _attn_kernel(*refs):
    q_ref, k_ref, v_ref = refs[:3]
    o_ref = refs[-1]
    lane = lax.broadcasted_iota(jnp.int32, (1, LANES), 1)
    for pair in range(N_HEADS // 2):
        outs = []
        for hd in (2 * pair, 2 * pair + 1):
            s = lax.dot_general(q_ref[hd], k_ref[hd], (((1,), (1,)), ((), ())),
                                preferred_element_type=F32)
            e = jnp.exp2(s - jnp.max(s, axis=-1, keepdims=True))
            o = _dot(e.astype(BF16), v_ref[hd])
            den = o[:, V_DIM:V_DIM + 1] if hd % 2 == 0 else o[:, 0:1]
            outs.append(o / den)
        o_ref[:, pair * LANES:(pair + 1) * LANES] = jnp.where(
            lane < V_DIM, outs[0], outs[1]).astype(BF16)


def _attention(q, k, v, q_start, q_len, key_lo, out_len, carry):
    b, _, s, _ = q.shape
    tq = next(t for t in ATTN_TILES if q_len % t == 0 and q_start % t == 0)
    base = q_start // tq
    n_keys = s - key_lo
    assert key_lo % n_keys == 0
    key_blk = key_lo // n_keys
    in_specs = [
        pl.BlockSpec((None, N_HEADS, tq, LANES), lambda bb, j: (bb, 0, base + j, 0)),
        pl.BlockSpec((None, N_HEADS, n_keys, LANES), lambda bb, j: (bb, 0, key_blk, 0)),
        pl.BlockSpec((None, N_HEADS, n_keys, LANES), lambda bb, j: (bb, 0, key_blk, 0)),
    ]
    args = [q, k, v]
    aliases = {}
    if carry is not None:
        in_specs.append(pl.BlockSpec(memory_space=pl.ANY))
        args.append(carry)
        aliases = {3: 0}
    return pl.pallas_call(
        _attn_kernel,
        grid=(b, q_len // tq),
        in_specs=in_specs,
        out_specs=pl.BlockSpec((None, tq, BRANCH_W), lambda bb, j: (bb, base + j, 0)),
        out_shape=jax.ShapeDtypeStruct((b, out_len, BRANCH_W), BF16),
        input_output_aliases=aliases,
        compiler_params=pltpu.CompilerParams(
            dimension_semantics=("arbitrary", "arbitrary"), vmem_limit_bytes=VMEM_LIMIT),
    )(*args)


def _route(logits, rbias):
    row = lax.broadcasted_iota(jnp.int32, logits.shape, 0)
    slot = row % EXPERTS_PER_GROUP
    s = jax.nn.sigmoid(logits)
    sb = s + rbias

    def mate(x, dd):
        fwd = pltpu.roll(x, N_EXPERTS - dd, 0)
        back = pltpu.roll(x, EXPERTS_PER_GROUP - dd, 0)
        return jnp.where(slot + dd < EXPERTS_PER_GROUP, fwd, back)

    mates = [mate(sb, dd) for dd in (1, 2, 3)]
    vals = [sb] + mates
    gscore = None
    for a in range(4):
        for c in range(a + 1, 4):
            pair = vals[a] + vals[c]
            gscore = pair if gscore is None else jnp.maximum(gscore, pair)
    group_ok = None
    for gg in (1, 2, 3):
        other = pltpu.roll(gscore, EXPERTS_PER_GROUP * gg, 0)
        other_is_lower = row >= EXPERTS_PER_GROUP * gg
        wins = jnp.logical_or(gscore > other,
                              jnp.logical_and(gscore == other, jnp.logical_not(other_is_lower)))
        group_ok = wins if group_ok is None else jnp.logical_and(group_ok, wins)
    rank = jnp.zeros(logits.shape, jnp.int32)
    for dd, mv in zip((1, 2, 3), mates):
        wrapped = slot + dd >= EXPERTS_PER_GROUP
        beats = jnp.logical_or(mv > sb, jnp.logical_and(wrapped, mv == sb))
        rank = rank + beats.astype(jnp.int32)
    chosen = jnp.logical_and(rank < 2, group_ok)
    w = jnp.where(chosen, s, 0.0)
    group_f = jnp.where(group_ok, 1.0, 0.0)
    group_onehot = jnp.concatenate(
        [group_f[gg * EXPERTS_PER_GROUP:gg * EXPERTS_PER_GROUP + 1]
         for gg in range(N_EXPERT_GROUPS)], axis=0)
    return w / jnp.sum(w, axis=0, keepdims=True), group_onehot


N_MIX_INPUTS = 19


def _mix_kernel(*refs, seg_tiles):
    (x_ref, xp_ref, xn_ref, mod_ref, g1_ref, g2_ref, wh_ref, wm_ref, attn_ref,
     convw_ref, poolw_ref, pscale_ref, gsgu_ref, sguw_ref, sgub_ref, wbr_ref, wo_ref,
     rw_ref, rb_ref) = refs[:N_MIX_INPUTS]
    xo_ref, h2_ref, cmb_ref, grp_ref, hcat_ref = refs[-5:]
    tm = x_ref.shape[0]
    j = pl.program_id(1)
    m = mod_ref[...]
    g1 = g1_ref[...]
    x = x_ref[...]

    def hnorm(xv):
        return _norm_mod(xv, g1, m[0:1], m[1:2]).astype(BF16)

    hm = hnorm(x)
    hcat_ref[0:HALO, :] = hnorm(xp_ref[...])
    hcat_ref[HALO:HALO + tm, :] = hm
    hcat_ref[HALO + tm:, :] = hnorm(xn_ref[...])

    nrow = tm + 2 * HALO
    row = lax.broadcasted_iota(jnp.int32, (nrow, 1), 0)
    keep = jnp.logical_and(jnp.logical_or(row >= HALO, j > 0),
                           jnp.logical_or(row < HALO + tm, j < seg_tiles - 1))
    ph = _dot(hcat_ref[...], wh_ref[...])
    ph = jnp.where(keep, ph, 0.0)

    def shifted(a, k):
        return pltpu.roll(a, (-k) % nrow, 0)[HALO:HALO + tm]

    cu = ph[:, :BRANCH_W] * ph[:, BRANCH_W:2 * BRANCH_W]
    cw = convw_ref[...]
    conv = shifted(cu, -1) * cw[0:1] + cu[HALO:HALO + tm] * cw[1:2] + shifted(cu, 1) * cw[2:3]
    gb = _dot(hm, wm_ref[:, 0:BRANCH_W])
    br_conv = gb * conv

    pp = ph[:, 2 * BRANCH_W:]
    seg_len = seg_tiles * tm
    pos = j * tm + lax.broadcasted_iota(jnp.int32, (tm, 1), 0)
    level = pp + pltpu.roll(pp, 1, 0)
    pooled = []
    for gi, win in enumerate(POOL_WINDOWS):
        if gi > 0:
            half = win // 4
            level = level[:, POOL_GC:]
            level = pltpu.roll(level, nrow - half, 0) + pltpu.roll(level, half, 0)
        cnt = (jnp.minimum(pos + win // 2, seg_len) - jnp.maximum(pos - win // 2, 0)).astype(F32)
        own = pp[HALO:HALO + tm, gi * POOL_GC:(gi + 1) * POOL_GC]
        pooled.append(level[HALO:HALO + tm, :POOL_GC] / cnt - own)
    br_pool = jnp.concatenate(
        [_dot(pooled[gi].astype(BF16), poolw_ref[gi]) for gi in range(len(POOL_WINDOWS))], axis=-1)
    br_pool = br_pool * pscale_ref[...]

    z = jax.nn.gelu(_dot(hm, wm_ref[:, BRANCH_W:3 * BRANCH_W]), approximate=True)
    zu = z[:, :BRANCH_W]
    zv = (_rms(z[:, BRANCH_W:]) * gsgu_ref[...]).astype(BF16)
    mixed = jnp.concatenate([
        jnp.concatenate([_dot(sguw_ref[gi], zv[r0:r0 + SGU_TILE, gi * LANES:(gi + 1) * LANES])
                         for gi in range(SGU_G)], axis=-1) + sgub_ref[...]
        for r0 in range(0, tm, SGU_TILE)], axis=0)
    br_sgu = zu * mixed

    branches = (attn_ref[...], br_conv.astype(BF16), br_pool.astype(BF16), br_sgu.astype(BF16))
    merged = None
    for nb, br in enumerate(branches):
        lo = 3 * BRANCH_W + nb * D_MODEL
        gate = jax.nn.sigmoid(_dot(hm, wm_ref[:, lo:lo + D_MODEL]))
        term = gate * _dot(br, wbr_ref[nb])
        merged = term if merged is None else merged + term
    x_new = x + m[2:3] * _dot(merged.astype(BF16), wo_ref[...])
    xo_ref[...] = x_new

    h2 = _norm_mod(x_new, g2_ref[...], m[3:4], m[4:5])
    h2_ref[...] = h2.astype(BF16)
    logits = lax.dot_general(rw_ref[...], h2, (((1,), (1,)), ((), ())),
                             preferred_element_type=F32, precision=lax.Precision.HIGHEST)
    cmb_t, grp_t = _route(logits, rb_ref[...])
    pad = jnp.zeros((LANES - N_EXPERTS - N_EXPERT_GROUPS, tm), F32)
    cmb_ref[...] = jnp.concatenate([cmb_t, grp_t, pad], axis=0).T
    grp_ref[...] = jnp.concatenate([grp_t, jnp.zeros_like(grp_t)], axis=0)


def _mix(xs, mod, attn, lw, i, seg_start, seg_len, seg_id, out_len, carry, router_w, router_b):
    b, s, d = xs.shape
    tm = next(t for t in MIX_TILES if seg_len % t == 0 and seg_start % t == 0)
    seg_tiles = seg_len // tm
    base = seg_start // tm
    hb = tm // HALO
    base_h = seg_start // HALO
    last_h = s // HALO - 1

    def cw(shape):
        return _const_spec((None,) + shape, lambda bb, j: (i,) + (0,) * len(shape))

    tok = lambda bb, j: (bb, base + j, 0)
    in_specs = [
        pl.BlockSpec((None, tm, d), tok),
        pl.BlockSpec((None, HALO, d), lambda bb, j: (bb, jnp.maximum(base_h + j * hb - 1, 0), 0)),
        pl.BlockSpec((None, HALO, d),
                     lambda bb, j: (bb, jnp.minimum(base_h + (j + 1) * hb, last_h), 0)),
        pl.BlockSpec((None, None, None, N_MOD, d), lambda bb, j: (i, bb, seg_id, 0, 0)),
        cw((1, d)), cw((1, d)),
        cw((d, 3 * BRANCH_W)), cw((d, 3 * BRANCH_W + N_BRANCH * D_MODEL)),
        pl.BlockSpec((None, tm, BRANCH_W), tok),
        cw((CONV_K, BRANCH_W)), cw((len(POOL_WINDOWS), POOL_GC, POOL_GC)), cw((1, BRANCH_W)),
        cw((1, BRANCH_W)), cw((SGU_G, SGU_TILE, SGU_TILE)), cw((SGU_TILE, BRANCH_W)),
        cw((N_BRANCH, BRANCH_W, D_MODEL)), cw((D_MODEL, D_MODEL)),
        _const_spec((N_EXPERTS, d), lambda bb, j: (0, 0)),
        _const_spec((N_EXPERTS, 1), lambda bb, j: (0, 0)),
    ]
    assert len(in_specs) == N_MIX_INPUTS
    args = [xs, xs, xs, mod, lw["g1"], lw["g2"], lw["w_h"], lw["w_m"], attn,
            lw["conv_w"], lw["pool_w"], lw["pool_scale"], lw["g_sgu"], lw["sgu_w"], lw["sgu_b"],
            lw["w_branch"], lw["w_o"], router_w, router_b]
    aliases = {}
    if carry is not None:
        in_specs += [pl.BlockSpec(memory_space=pl.ANY)] * len(carry)
        args += list(carry)
        aliases = {N_MIX_INPUTS + n: n for n in range(len(carry))}
    return pl.pallas_call(
        functools.partial(_mix_kernel, seg_tiles=seg_tiles),
        grid=(b, seg_tiles),
        in_specs=in_specs,
        out_specs=[
            pl.BlockSpec((None, tm, d), tok),
            pl.BlockSpec((None, tm, d), tok),
            pl.BlockSpec((None, tm, LANES), tok),
            pl.BlockSpec((None, 2 * N_EXPERT_GROUPS, tm), lambda bb, j: (bb, 0, base + j)),
        ],
        out_shape=[
            jax.ShapeDtypeStruct((b, out_len, d), F32),
            jax.ShapeDtypeStruct((b, out_len, d), BF16),
            jax.ShapeDtypeStruct((b, out_len, LANES), F32),
            jax.ShapeDtypeStruct((b, 2 * N_EXPERT_GROUPS, out_len), F32),
        ],
        input_output_aliases=aliases,
        scratch_shapes=[pltpu.VMEM((tm + 2 * HALO, d), BF16)],
        compiler_params=pltpu.CompilerParams(
            dimension_semantics=("arbitrary", "arbitrary"), vmem_limit_bytes=VMEM_LIMIT),
    )(*args)


def _moe_kernel(x_ref, h_ref, cmb_ref, grp_ref, mod_ref, before_ref, after_ref,
                w1_ref, w3_ref, w2_ref, o_ref, rank_col, rank_row, group_cmb, count, *, chunk, tail, lat_rows):
    tile = x_ref.shape[0]
    g = pl.program_id(2)
    n_exp = w1_ref.shape[0]

    @pl.when(g == 0)
    def _():
        o_ref[...] = x_ref[...]
        lane = lax.broadcasted_iota(jnp.int32, (1, LANES), 1)
        is_group = jnp.logical_and(lane >= N_EXPERTS, lane < N_EXPERTS + N_EXPERT_GROUPS)
        member_c = jnp.where(is_group, cmb_ref[...], 0.0)
        member_r = grp_ref[...]
        earlier_c = _dot(before_ref[...], member_c.astype(BF16))
        earlier_r = _dot(member_r.astype(BF16), after_ref[...])
        for gg in range(N_EXPERT_GROUPS):
            col = slice(N_EXPERTS + gg, N_EXPERTS + gg + 1)
            rank_col[gg] = jnp.broadcast_to(
                jnp.where(member_c[:, col] > 0.0, earlier_c[:, col], -1.0), (tile, LANES))
            rank_row[gg] = jnp.broadcast_to(
                jnp.where(member_r[gg:gg + 1] > 0.0, earlier_r[gg:gg + 1], -1.0), rank_row.shape[1:])
            group_cmb[gg] = cmb_ref[:, gg * EXPERTS_PER_GROUP:(gg + 1) * EXPERTS_PER_GROUP]
            count[gg] = jnp.sum(member_r[gg:gg + 1]).astype(jnp.int32)

    gate = mod_ref[1, N_MOD - 1:N_MOD]
    if lat_rows is not None:
        pos = pl.program_id(1) * tile + lax.broadcasted_iota(jnp.int32, (tile, 1), 0)
        gate = jnp.where(pos < lat_rows, gate, mod_ref[0, N_MOD - 1:N_MOD])
    cg = group_cmb[g]
    cg_hi = cg.astype(BF16)
    cg_lo = (cg - cg_hi.astype(F32)).astype(BF16)
    rrow = rank_row[g][0:1, :]
    rcol = rank_col[g]

    def run_chunk(first_rank, size):
        first = first_rank.astype(F32)
        want_r = lax.broadcasted_iota(jnp.int32, (size, 1), 0).astype(F32) + first
        gather = jnp.where(rrow == want_r, 1.0, 0.0).astype(BF16)
        h = _dot(gather, h_ref[...]).astype(BF16)
        cm = _dot(gather, cg_hi) + _dot(gather, cg_lo)
        acts = []
        for e in range(n_exp):
            a = _dot(h, w1_ref[e])
            a = (a * jax.nn.sigmoid(a)) * _dot(h, w3_ref[e])
            acts.append((a * cm[:, e:e + 1]).astype(BF16))
        y = _dot(jnp.concatenate(acts, axis=-1), w2_ref[...]).astype(BF16)
        want_c = lax.broadcasted_iota(jnp.int32, (1, size), 1).astype(F32) + first
        rank_lanes = jnp.concatenate([rcol] * (size // LANES), axis=1)
        scatter = jnp.where(rank_lanes == want_c, 1.0, 0.0).astype(BF16)
        o_ref[...] += gate * _dot(scatter, y)

    cnt = count[g]
    left = cnt % chunk
    n_full = cnt // chunk + jnp.where(left > tail, 1, 0)

    def full_chunk(c, carry):
        run_chunk(c * chunk, chunk)
        return carry

    lax.fori_loop(0, n_full, full_chunk, 0)

    @pl.when(jnp.logical_and(left > 0, left <= tail))
    def _():
        run_chunk(n_full * chunk, tail)


def _moe_tile(s):
    for tile, chunk in ((768, 256), (1024, 384), (512, 256), (256, 128)):
        if s % tile == 0:
            return tile, chunk
    raise ValueError(f"unsupported sequence length {s}")


def _moe(x_new, h2, cmb, grp, mod, lw, i, lat_rows):
    b, s, d = x_new.shape
    tile, chunk = _moe_tile(s)
    ge = EXPERTS_PER_GROUP
    tpos = jnp.arange(tile)
    before = (tpos[None, :] < tpos[:, None]).astype(BF16)
    tok = lambda bb, j, e: (bb, j, 0)
    kern = functools.partial(_moe_kernel, chunk=chunk, tail=LANES, lat_rows=lat_rows)
    return pl.pallas_call(
        kern,
        grid=(b, s // tile, N_EXPERT_GROUPS),
        in_specs=[
            pl.BlockSpec((None, tile, d), tok),
            pl.BlockSpec((None, tile, d), tok),
            pl.BlockSpec((None, tile, LANES), tok),
            pl.BlockSpec((None, 2 * N_EXPERT_GROUPS, tile), lambda bb, j, e: (bb, 0, j)),
            pl.BlockSpec((None, None, 2, N_MOD, d), lambda bb, j, e: (i, bb, 0, 0, 0)),
            _const_spec((tile, tile), lambda bb, j, e: (0, 0)),
            _const_spec((tile, tile), lambda bb, j, e: (0, 0)),
            pl.BlockSpec((None, ge, d, EXPERT_FF), lambda bb, j, e: (i, e, 0, 0)),
            pl.BlockSpec((None, ge, d, EXPERT_FF), lambda bb, j, e: (i, e, 0, 0)),
            pl.BlockSpec((None, ge * EXPERT_FF, d), lambda bb, j, e: (i, e, 0)),
        ],
        out_specs=pl.BlockSpec((None, tile, d), tok),
        out_shape=jax.ShapeDtypeStruct((b, s, d), F32),
        input_output_aliases={0: 0},
        scratch_shapes=[
            pltpu.VMEM((N_EXPERT_GROUPS, tile, LANES), F32),
            pltpu.VMEM((N_EXPERT_GROUPS, 8, tile), F32),
            pltpu.VMEM((N_EXPERT_GROUPS, tile, ge), F32),
            pltpu.SMEM((N_EXPERT_GROUPS,), jnp.int32),
        ],
        compiler_params=pltpu.CompilerParams(
            dimension_semantics=("arbitrary", "arbitrary", "arbitrary"), vmem_limit_bytes=VMEM_LIMIT),
    )(x_new, h2, cmb, grp, mod, before, before.T, lw["w1"], lw["w3"], lw["w2"])


def _prepare(p):
    dpt = p["w_in"].shape[0]
    w_in = p["w_in"].astype(BF16)
    zeros = lambda n: jnp.zeros((dpt, D_MODEL, n), BF16)
    kr_mid = KR_OFF + ROPE_HALF
    w_a = jnp.concatenate([
        w_in[..., KV_OFF:KR_OFF],
        zeros(NOPE_DIM), w_in[..., KR_OFF:Q_OFF], zeros(LANES - QK_DIM),
        zeros(NOPE_DIM), w_in[..., kr_mid:Q_OFF], w_in[..., KR_OFF:kr_mid], zeros(LANES - QK_DIM),
        w_in[..., Q_OFF:CONV_OFF]], axis=-1)
    gb_lo, gc_lo, u_lo = CONV_OFF, CONV_OFF + BRANCH_W, CONV_OFF + 2 * BRANCH_W
    w_h = jnp.concatenate([w_in[..., gc_lo:u_lo], w_in[..., u_lo:POOL_OFF],
                           w_in[..., POOL_OFF:SGU_OFF]], axis=-1)
    w_m = jnp.concatenate([w_in[..., gb_lo:gc_lo], w_in[..., SGU_OFF:]], axis=-1)
    w_ukv = p["w_ukv"].astype(BF16)
    w_uk = jnp.concatenate([w_ukv[..., :NOPE_DIM], jnp.zeros_like(w_ukv[..., :NOPE_DIM])], axis=-1)
    w_v = w_ukv[..., NOPE_DIM:]
    even = (jnp.arange(N_HEADS) % 2 == 0)[None, None, :, None]
    zero_v = jnp.zeros_like(w_v)
    w_uv = jnp.concatenate([jnp.where(even, w_v, zero_v), jnp.where(even, zero_v, w_v)], axis=-1)
    w_q = p["w_uq"].astype(BF16)
    w_uq = jnp.pad(w_q, ((0, 0), (0, 0), (0, 0), (0, LANES - QK_DIM)))
    rope_mid = NOPE_DIM + ROPE_HALF
    w_uqs = jnp.concatenate([
        jnp.zeros_like(w_q[..., :NOPE_DIM]), w_q[..., rope_mid:], w_q[..., NOPE_DIM:rope_mid],
        jnp.zeros_like(w_q[..., :LANES - QK_DIM])], axis=-1)
    eye = jnp.eye(SGU_TILE // SGU_CHUNK, dtype=F32)
    sgu_bd = jnp.einsum("ab,lgpq->lgapbq", eye, p["sgu_w"]).reshape(dpt, SGU_G, SGU_TILE, SGU_TILE)
    sgu_b = jnp.repeat(jnp.swapaxes(p["sgu_b"], 1, 2), SGU_CHUNK, axis=2)
    sgu_b = jnp.tile(sgu_b, (1, SGU_TILE // SGU_CHUNK, 1))
    return {
        "g1": p["norm1_g"][:, None, :], "g2": p["norm2_g"][:, None, :],
        "w_a": w_a, "w_h": w_h, "w_m": w_m,
        "g_kv": p["kv_a_norm_g"][:, None, :], "g_qa": p["q_a_norm_g"][:, None, :],
        "w_uk": w_uk.reshape(dpt, KV_LORA, N_HEADS * LANES),
        "w_uv": w_uv.reshape(dpt, KV_LORA, N_HEADS * LANES),
        "w_uq": w_uq.reshape(dpt, Q_LORA, N_HEADS * LANES),
        "w_uqs": w_uqs.reshape(dpt, Q_LORA, N_HEADS * LANES),
        "conv_w": p["conv_w"], "pool_w": p["pool_w"].astype(BF16),
        "pool_scale": p["pool_scale"][:, None, :], "g_sgu": p["sgu_norm_g"][:, None, :],
        "sgu_w": sgu_bd.astype(BF16), "sgu_b": sgu_b,
        "w_branch": p["w_branch"].astype(BF16), "w_o": p["w_o"].astype(BF16),
        "w1": p["moe_w1"].astype(BF16), "w3": p["moe_w3"].astype(BF16),
        "w2": p["moe_w2"].reshape(dpt, N_EXPERTS * EXPERT_FF, D_MODEL).astype(BF16),
    }


def _rope_tables(ctx_len, length, k_gain, q_gain):
    rows = length // GRID_W
    t = jnp.arange(rows * GRID_W)
    grid = jnp.stack([t // GRID_W, t % GRID_W], -1).astype(F32)
    inv_freq = ROPE_THETA ** (-jnp.arange(AXIS_PAIRS, dtype=F32) / AXIS_PAIRS)
    ang = jnp.concatenate([grid[:, :1] * inv_freq, grid[:, 1:] * inv_freq], -1)
    cos, sin = jnp.cos(ang), jnp.sin(ang)
    one = jnp.ones((length, NOPE_DIM), F32)
    tail = jnp.ones((length, LANES - QK_DIM), F32)
    zn = jnp.zeros((length, NOPE_DIM), F32)
    zt = jnp.zeros((length, LANES - QK_DIM), F32)
    rc = jnp.concatenate([one, cos, cos, tail], -1)
    rs = jnp.concatenate([zn, -sin, sin, zt], -1)
    ctx_id = lambda fill: jnp.full((ctx_len, LANES), fill, F32)
    rc = jnp.concatenate([rc, ctx_id(1.0)], 0)
    rs = jnp.concatenate([rs, ctx_id(0.0)], 0)

    def tables(gain, const):
        g = jnp.pad(gain, ((0, 0), (0, LANES - QK_DIM)))
        mid = NOPE_DIM + ROPE_HALF
        g_swap = jnp.concatenate([g[:, :NOPE_DIM], g[:, mid:QK_DIM], g[:, NOPE_DIM:mid],
                                  g[:, QK_DIM:]], -1)
        return (g[:, None, :] * rc[None]) * const, (g_swap[:, None, :] * rs[None]) * const

    gak, gbk = tables(k_gain, 1.0)
    gaq, gbq = tables(q_gain, ATTN_SCALE * LOG2E)
    return gak, gbk, gaq, gbq


def kernel(x, c, ctx, c_ctx, mod_w, mod_b, norm1_g, norm2_g, w_in, kv_a_norm_g, w_ukv, k_norm_g,
           q_a_norm_g, w_uq, q_norm_g, conv_w, pool_w, pool_scale, sgu_norm_g, sgu_w, sgu_b,
           w_branch, w_o, router_w, router_b, moe_w1, moe_w3, moe_w2):
    b, length, d = x.shape
    ctx_len = ctx.shape[1]
    assert d == D_MODEL and length % TOKEN_TILE == 0 and ctx_len % TOKEN_TILE == 0
    assert length % GRID_W == 0
    s = length + ctx_len

    lw = _prepare(dict(
        norm1_g=norm1_g, norm2_g=norm2_g, w_in=w_in, kv_a_norm_g=kv_a_norm_g, w_ukv=w_ukv,
        k_norm_g=k_norm_g, q_a_norm_g=q_a_norm_g, w_uq=w_uq, q_norm_g=q_norm_g, conv_w=conv_w,
        pool_w=pool_w, pool_scale=pool_scale, sgu_norm_g=sgu_norm_g, sgu_w=sgu_w, sgu_b=sgu_b,
        w_branch=w_branch, w_o=w_o, moe_w1=moe_w1, moe_w3=moe_w3, moe_w2=moe_w2))
    rope_tabs = _rope_tables(ctx_len, length, k_norm_g, q_norm_g)
    rw = router_w.T
    rb = router_b[:, None]

    rows = -(-(b + 1) // 8) * 8
    cvec = jnp.concatenate([c, c_ctx[None, :], jnp.zeros((rows - b - 1, d), F32)], 0)
    mod = _modulation(cvec, mod_w, mod_b).reshape(DEPTH, rows, N_MOD, d)
    mod = jnp.stack([jnp.broadcast_to(mod[:, b:b + 1], (DEPTH, b, N_MOD, d)), mod[:, :b]], axis=2)

    xs = jnp.concatenate([x, ctx], axis=1)
    for i in range(DEPTH):
        last = i == DEPTH - 1
        q, k, v = _qkv(xs, mod, lw, i, rope_tabs, length)
        out_len = length if last else s
        attn = _attention(q, k, v, 0, length, 0, out_len, None)
        if not last:
            attn = _attention(q, k, v, length, ctx_len, length, out_len, attn)
        outs = _mix(xs, mod, attn, lw, i, 0, length, 1, out_len, None, rw, rb)
        if not last:
            outs = _mix(xs, mod, attn, lw, i, length, ctx_len, 0, out_len, outs, rw, rb)
        x_new, h2, cmb, grp = outs
        xs = _moe(x_new, h2, cmb, grp, mod, lw, i, None if last else length)
    return xs
```

```python
import functools

import jax
import jax.numpy as jnp
from jax import lax
from jax.experimental import pallas as pl
from jax.experimental.pallas import tpu as pltpu

F32 = jnp.float32
BF16 = jnp.bfloat16

D_MODEL = 1024
DEPTH = 4
GRID_W = 64
EPS = 1e-6
N_MOD = 6
N_BRANCH = 4
BRANCH_W = 512
N_HEADS = 8
Q_LORA = 256
KV_LORA = 128
NOPE_DIM = 64
ROPE_DIM = 32
V_DIM = 64
QK_DIM = NOPE_DIM + ROPE_DIM
ROPE_HALF = ROPE_DIM // 2
AXIS_PAIRS = ROPE_DIM // 4
ROPE_THETA = 10000.0
ATTN_SCALE = QK_DIM ** -0.5
LOG2E = 1.4426950408889634
CONV_K = 3
POOL_WINDOWS = (2, 4, 8, 16)
POOL_GC = 128
SGU_CHUNK = 128
SGU_G = 4
N_EXPERTS = 16
N_EXPERT_GROUPS = 4
EXPERTS_PER_GROUP = 4
EXPERT_FF = 256

KV_OFF = 0
KR_OFF = KV_OFF + KV_LORA
Q_OFF = KR_OFF + ROPE_DIM
CONV_OFF = Q_OFF + Q_LORA
POOL_OFF = CONV_OFF + 3 * BRANCH_W
SGU_OFF = POOL_OFF + BRANCH_W
GATE_OFF = SGU_OFF + 2 * BRANCH_W

LANES = 128
HALO = 16
TOKEN_TILE = 256
SGU_TILE = 256
MIX_TILES = (512, 256)
ATTN_TILES = (512, 256)
VMEM_LIMIT = 56 * 1024 * 1024


def _rms(x, width=None):
    width = x.shape[-1] if width is None else width
    return x * lax.rsqrt(jnp.sum(x * x, axis=-1, keepdims=True) * (1.0 / width) + EPS)


def _norm_mod(x, g, shift, scale):
    return _rms(x) * (g * (1.0 + scale)) + shift


def _dot(a, b):
    return jnp.dot(a, b, preferred_element_type=F32)


def _const_spec(shape, index_map):
    return pl.BlockSpec(shape, index_map, pipeline_mode=pl.Buffered(1))


def _mod_kernel(c_ref, w_ref, b_ref, o_ref):
    c = c_ref[...]
    sc = c * jax.nn.sigmoid(c)
    o_ref[...] = jnp.dot(sc, w_ref[...], preferred_element_type=F32,
                         precision=lax.Precision.HIGHEST) + b_ref[...]


def _modulation(cvec, mod_w, mod_b):
    rows = cvec.shape[0]
    n = mod_w.shape[-1]
    bn = 1536
    return pl.pallas_call(
        _mod_kernel,
        grid=(DEPTH, n // bn),
        in_specs=[
            pl.BlockSpec((rows, D_MODEL), lambda i, j: (0, 0)),
            pl.BlockSpec((None, D_MODEL, bn), lambda i, j: (i, 0, j)),
            pl.BlockSpec((None, 1, bn), lambda i, j: (i, 0, j)),
        ],
        out_specs=pl.BlockSpec((None, rows, bn), lambda i, j: (i, 0, j)),
        out_shape=jax.ShapeDtypeStruct((DEPTH, rows, n), F32),
        compiler_params=pltpu.CompilerParams(
            dimension_semantics=("arbitrary", "arbitrary"), vmem_limit_bytes=VMEM_LIMIT),
    )(cvec, mod_w, mod_b.reshape(DEPTH, 1, n))


def _qkv_kernel(x_ref, mod_ref, g1_ref, wa_ref, gkv_ref, gq_ref, wuk_ref, wuv_ref, wuq_ref, wuqs_ref,
                gak_ref, gbk_ref, gaq_ref, gbq_ref, q_ref, k_ref, v_ref):
    x = x_ref[...]
    m = mod_ref[...]
    h = _norm_mod(x, g1_ref[...], m[0:1], m[1:2]).astype(BF16)
    p = _dot(h, wa_ref[...])
    ckv = (_rms(p[:, :KV_LORA]) * gkv_ref[...]).astype(BF16)
    kk = _dot(ckv, wuk_ref[...])
    vv = _dot(ckv, wuv_ref[...])
    kr = p[:, LANES:2 * LANES]
    krs = p[:, 2 * LANES:3 * LANES]
    cq = (_rms(p[:, 3 * LANES:]) * gq_ref[...]).astype(BF16)
    qq = _dot(cq, wuq_ref[...])
    qs = _dot(cq, wuqs_ref[...])
    gak, gbk, gaq, gbq = gak_ref[...], gbk_ref[...], gaq_ref[...], gbq_ref[...]
    k_shared = kr * gak + krs * gbk
    kr_sq = jnp.sum(kr * kr, axis=-1, keepdims=True)
    lane = lax.broadcasted_iota(jnp.int32, (1, LANES), 1)
    one_col = [jnp.where(lane == V_DIM, 1.0, 0.0), jnp.where(lane == 0, 1.0, 0.0)]

    for hd in range(N_HEADS):
        cols = slice(hd * LANES, (hd + 1) * LANES)
        kn = kk[:, cols]
        rk = lax.rsqrt((jnp.sum(kn * kn, axis=-1, keepdims=True) + kr_sq) * (1.0 / QK_DIM) + EPS)
        k_ref[hd] = ((kn * gak + k_shared) * rk).astype(BF16)
        qh = qq[:, cols]
        rq = lax.rsqrt(jnp.sum(qh * qh, axis=-1, keepdims=True) * (1.0 / QK_DIM) + EPS)
        q_ref[hd] = ((qh * gaq + qs[:, cols] * gbq) * rq).astype(BF16)
        v_ref[hd] = (vv[:, cols] + one_col[hd % 2]).astype(BF16)


def _qkv(xs, mod, lw, i, rope_tabs, lat_len):
    b, s, d = xs.shape
    tm = TOKEN_TILE
    n_lat_tiles = lat_len // tm

    def cw(shape):
        return _const_spec((None,) + shape, lambda bb, j: (i,) + (0,) * len(shape))

    head = lambda bb, j: (bb, 0, j, 0)
    rspec = pl.BlockSpec((None, tm, LANES), lambda bb, j: (i, j, 0))
    hspec = pl.BlockSpec((None, N_HEADS, tm, LANES), head)
    hshape = jax.ShapeDtypeStruct((b, N_HEADS, s, LANES), BF16)
    return pl.pallas_call(
        _qkv_kernel,
        grid=(b, s // tm),
        in_specs=[
            pl.BlockSpec((None, tm, d), lambda bb, j: (bb, j, 0)),
            pl.BlockSpec((None, None, None, N_MOD, d),
                         lambda bb, j: (i, bb, jnp.where(j < n_lat_tiles, 1, 0), 0, 0)),
            cw((1, d)), cw((d, 5 * LANES)), cw((1, KV_LORA)), cw((1, Q_LORA)),
            cw((KV_LORA, N_HEADS * LANES)), cw((KV_LORA, N_HEADS * LANES)),
            cw((Q_LORA, N_HEADS * LANES)), cw((Q_LORA, N_HEADS * LANES)),
            rspec, rspec, rspec, rspec,
        ],
        out_specs=[hspec, hspec, hspec],
        out_shape=[hshape, hshape, hshape],
        compiler_params=pltpu.CompilerParams(
            dimension_semantics=("arbitrary", "arbitrary"), vmem_limit_bytes=VMEM_LIMIT),
    )(xs, mod, lw["g1"], lw["w_a"], lw["g_kv"], lw["g_qa"], lw["w_uk"], lw["w_uv"], lw["w_uq"],
      lw["w_uqs"], *rope_tabs)


def _attn_kernel(*refs):
    q_ref, k_ref, v_ref = refs[:3]
    o_ref = refs[-1]
    lane = lax.broadcasted_iota(jnp.int32, (1, LANES), 1)
    for pair in range(N_HEADS // 2):
        outs = []
        for hd in (2 * pair, 2 * pair + 1):
            s = lax.dot_general(q_ref[hd], k_ref[hd], (((1,), (1,)), ((), ())),
                                preferred_element_type=F32)
            e = jnp.exp2(s - jnp.max(s, axis=-1, keepdims=True))
            o = _dot(e.astype(BF16), v_ref[hd])
            den = o[:, V_DIM:V_DIM + 1] if hd % 2 == 0 else o[:, 0:1]
            outs.append(o / den)
        o_ref[:, pair * LANES:(pair + 1) * LANES] = jnp.where(
            lane < V_DIM, outs[0], outs[1]).astype(BF16)


def _attention(q, k, v, q_start, q_len, key_lo, out_len, carry):
    b, _, s, _ = q.shape
    tq = next(t for t in ATTN_TILES if q_len % t == 0 and q_start % t == 0)
    base = q_start // tq
    n_keys = s - key_lo
    assert key_lo % n_keys == 0
    key_blk = key_lo // n_keys
    in_specs = [
        pl.BlockSpec((None, N_HEADS, tq, LANES), lambda bb, j: (bb, 0, base + j, 0)),
        pl.BlockSpec((None, N_HEADS, n_keys, LANES), lambda bb, j: (bb, 0, key_blk, 0)),
        pl.BlockSpec((None, N_HEADS, n_keys, LANES), lambda bb, j: (bb, 0, key_blk, 0)),
    ]
    args = [q, k, v]
    aliases = {}
    if carry is not None:
        in_specs.append(pl.BlockSpec(memory_space=pl.ANY))
        args.append(carry)
        aliases = {3: 0}
    return pl.pallas_call(
        _attn_kernel,
        grid=(b, q_len // tq),
        in_specs=in_specs,
        out_specs=pl.BlockSpec((None, tq, BRANCH_W), lambda bb, j: (bb, base + j, 0)),
        out_shape=jax.ShapeDtypeStruct((b, out_len, BRANCH_W), BF16),
        input_output_aliases=aliases,
        compiler_params=pltpu.CompilerParams(
            dimension_semantics=("arbitrary", "arbitrary"), vmem_limit_bytes=VMEM_LIMIT),
    )(*args)


def _route(logits, rbias):
    row = lax.broadcasted_iota(jnp.int32, logits.shape, 0)
    slot = row % EXPERTS_PER_GROUP
    s = jax.nn.sigmoid(logits)
    sb = s + rbias

    def mate(x, dd):
        fwd = pltpu.roll(x, N_EXPERTS - dd, 0)
        back = pltpu.roll(x, EXPERTS_PER_GROUP - dd, 0)
        return jnp.where(slot + dd < EXPERTS_PER_GROUP, fwd, back)

    mates = [mate(sb, dd) for dd in (1, 2, 3)]
    vals = [sb] + mates
    gscore = None
    for a in range(4):
        for c in range(a + 1, 4):
            pair = vals[a] + vals[c]
            gscore = pair if gscore is None else jnp.maximum(gscore, pair)
    group_ok = None
    for gg in (1, 2, 3):
        other = pltpu.roll(gscore, EXPERTS_PER_GROUP * gg, 0)
        other_is_lower = row >= EXPERTS_PER_GROUP * gg
        wins = jnp.logical_or(gscore > other,
                              jnp.logical_and(gscore == other, jnp.logical_not(other_is_lower)))
        group_ok = wins if group_ok is None else jnp.logical_and(group_ok, wins)
    rank = jnp.zeros(logits.shape, jnp.int32)
    for dd, mv in zip((1, 2, 3), mates):
        wrapped = slot + dd >= EXPERTS_PER_GROUP
        beats = jnp.logical_or(mv > sb, jnp.logical_and(wrapped, mv == sb))
        rank = rank + beats.astype(jnp.int32)
    chosen = jnp.logical_and(rank < 2, group_ok)
    w = jnp.where(chosen, s, 0.0)
    group_f = jnp.where(group_ok, 1.0, 0.0)
    group_onehot = jnp.concatenate(
        [group_f[gg * EXPERTS_PER_GROUP:gg * EXPERTS_PER_GROUP + 1]
         for gg in range(N_EXPERT_GROUPS)], axis=0)
    return w / jnp.sum(w, axis=0, keepdims=True), group_onehot


N_MIX_INPUTS = 19


def _mix_kernel(*refs, seg_tiles):
    (x_ref, xp_ref, xn_ref, mod_ref, g1_ref, g2_ref, wh_ref, wm_ref, attn_ref,
     convw_ref, poolw_ref, pscale_ref, gsgu_ref, sguw_ref, sgub_ref, wbr_ref, wo_ref,
     rw_ref, rb_ref) = refs[:N_MIX_INPUTS]
    xo_ref, h2_ref, cmb_ref, grp_ref, hcat_ref = refs[-5:]
    tm = x_ref.shape[0]
    j = pl.program_id(1)
    m = mod_ref[...]
    g1 = g1_ref[...]
    x = x_ref[...]

    def hnorm(xv):
        return _norm_mod(xv, g1, m[0:1], m[1:2]).astype(BF16)

    hm = hnorm(x)
    hcat_ref[0:HALO, :] = hnorm(xp_ref[...])
    hcat_ref[HALO:HALO + tm, :] = hm
    hcat_ref[HALO + tm:, :] = hnorm(xn_ref[...])

    nrow = tm + 2 * HALO
    row = lax.broadcasted_iota(jnp.int32, (nrow, 1), 0)
    keep = jnp.logical_and(jnp.logical_or(row >= HALO, j > 0),
                           jnp.logical_or(row < HALO + tm, j < seg_tiles - 1))
    ph = _dot(hcat_ref[...], wh_ref[...])
    ph = jnp.where(keep, ph, 0.0)

    def shifted(a, k):
        return pltpu.roll(a, (-k) % nrow, 0)[HALO:HALO + tm]

    cu = ph[:, :BRANCH_W] * ph[:, BRANCH_W:2 * BRANCH_W]
    cw = convw_ref[...]
    conv = shifted(cu, -1) * cw[0:1] + cu[HALO:HALO + tm] * cw[1:2] + shifted(cu, 1) * cw[2:3]
    gb = _dot(hm, wm_ref[:, 0:BRANCH_W])
    br_conv = gb * conv

    pp = ph[:, 2 * BRANCH_W:]
    seg_len = seg_tiles * tm
    pos = j * tm + lax.broadcasted_iota(jnp.int32, (tm, 1), 0)
    level = pp + pltpu.roll(pp, 1, 0)
    pooled = []
    for gi, win in enumerate(POOL_WINDOWS):
        if gi > 0:
            half = win // 4
            level = level[:, POOL_GC:]
            level = pltpu.roll(level, nrow - half, 0) + pltpu.roll(level, half, 0)
        cnt = (jnp.minimum(pos + win // 2, seg_len) - jnp.maximum(pos - win // 2, 0)).astype(F32)
        own = pp[HALO:HALO + tm, gi * POOL_GC:(gi + 1) * POOL_GC]
        pooled.append(level[HALO:HALO + tm, :POOL_GC] / cnt - own)
    br_pool = jnp.concatenate(
        [_dot(pooled[gi].astype(BF16), poolw_ref[gi]) for gi in range(len(POOL_WINDOWS))], axis=-1)
    br_pool = br_pool * pscale_ref[...]

    z = jax.nn.gelu(_dot(hm, wm_ref[:, BRANCH_W:3 * BRANCH_W]), approximate=True)
    zu = z[:, :BRANCH_W]
    zv = (_rms(z[:, BRANCH_W:]) * gsgu_ref[...]).astype(BF16)
    mixed = jnp.concatenate([
        jnp.concatenate([_dot(sguw_ref[gi], zv[r0:r0 + SGU_TILE, gi * LANES:(gi + 1) * LANES])
                         for gi in range(SGU_G)], axis=-1) + sgub_ref[...]
        for r0 in range(0, tm, SGU_TILE)], axis=0)
    br_sgu = zu * mixed

    branches = (attn_ref[...], br_conv.astype(BF16), br_pool.astype(BF16), br_sgu.astype(BF16))
    merged = None
    for nb, br in enumerate(branches):
        lo = 3 * BRANCH_W + nb * D_MODEL
        gate = jax.nn.sigmoid(_dot(hm, wm_ref[:, lo:lo + D_MODEL]))
        term = gate * _dot(br, wbr_ref[nb])
        merged = term if merged is None else merged + term
    x_new = x + m[2:3] * _dot(merged.astype(BF16), wo_ref[...])
    xo_ref[...] = x_new

    h2 = _norm_mod(x_new, g2_ref[...], m[3:4], m[4:5])
    h2_ref[...] = h2.astype(BF16)
    logits = lax.dot_general(rw_ref[...], h2, (((1,), (1,)), ((), ())),
                             preferred_element_type=F32, precision=lax.Precision.HIGHEST)
    cmb_t, grp_t = _route(logits, rb_ref[...])
    pad = jnp.zeros((LANES - N_EXPERTS - N_EXPERT_GROUPS, tm), F32)
    cmb_ref[...] = jnp.concatenate([cmb_t, grp_t, pad], axis=0).T
    grp_ref[...] = jnp.concatenate([grp_t, jnp.zeros_like(grp_t)], axis=0)


def _mix(xs, mod, attn, lw, i, seg_start, seg_len, seg_id, out_len, carry, router_w, router_b):
    b, s, d = xs.shape
    tm = next(t for t in MIX_TILES if seg_len % t == 0 and seg_start % t == 0)
    seg_tiles = seg_len // tm
    base = seg_start // tm
    hb = tm // HALO
    base_h = seg_start // HALO
    last_h = s // HALO - 1

    def cw(shape):
        return _const_spec((None,) + shape, lambda bb, j: (i,) + (0,) * len(shape))

    tok = lambda bb, j: (bb, base + j, 0)
    in_specs = [
        pl.BlockSpec((None, tm, d), tok),
        pl.BlockSpec((None, HALO, d), lambda bb, j: (bb, jnp.maximum(base_h + j * hb - 1, 0), 0)),
        pl.BlockSpec((None, HALO, d),
                     lambda bb, j: (bb, jnp.minimum(base_h + (j + 1) * hb, last_h), 0)),
        pl.BlockSpec((None, None, None, N_MOD, d), lambda bb, j: (i, bb, seg_id, 0, 0)),
        cw((1, d)), cw((1, d)),
        cw((d, 3 * BRANCH_W)), cw((d, 3 * BRANCH_W + N_BRANCH * D_MODEL)),
        pl.BlockSpec((None, tm, BRANCH_W), tok),
        cw((CONV_K, BRANCH_W)), cw((len(POOL_WINDOWS), POOL_GC, POOL_GC)), cw((1, BRANCH_W)),
        cw((1, BRANCH_W)), cw((SGU_G, SGU_TILE, SGU_TILE)), cw((SGU_TILE, BRANCH_W)),
        cw((N_BRANCH, BRANCH_W, D_MODEL)), cw((D_MODEL, D_MODEL)),
        _const_spec((N_EXPERTS, d), lambda bb, j: (0, 0)),
        _const_spec((N_EXPERTS, 1), lambda bb, j: (0, 0)),
    ]
    assert len(in_specs) == N_MIX_INPUTS
    args = [xs, xs, xs, mod, lw["g1"], lw["g2"], lw["w_h"], lw["w_m"], attn,
            lw["conv_w"], lw["pool_w"], lw["pool_scale"], lw["g_sgu"], lw["sgu_w"], lw["sgu_b"],
            lw["w_branch"], lw["w_o"], router_w, router_b]
    aliases = {}
    if carry is not None:
        in_specs += [pl.BlockSpec(memory_space=pl.ANY)] * len(carry)
        args += list(carry)
        aliases = {N_MIX_INPUTS + n: n for n in range(len(carry))}
    return pl.pallas_call(
        functools.partial(_mix_kernel, seg_tiles=seg_tiles),
        grid=(b, seg_tiles),
        in_specs=in_specs,
        out_specs=[
            pl.BlockSpec((None, tm, d), tok),
            pl.BlockSpec((None, tm, d), tok),
            pl.BlockSpec((None, tm, LANES), tok),
            pl.BlockSpec((None, 2 * N_EXPERT_GROUPS, tm), lambda bb, j: (bb, 0, base + j)),
        ],
        out_shape=[
            jax.ShapeDtypeStruct((b, out_len, d), F32),
            jax.ShapeDtypeStruct((b, out_len, d), BF16),
            jax.ShapeDtypeStruct((b, out_len, LANES), F32),
            jax.ShapeDtypeStruct((b, 2 * N_EXPERT_GROUPS, out_len), F32),
        ],
        input_output_aliases=aliases,
        scratch_shapes=[pltpu.VMEM((tm + 2 * HALO, d), BF16)],
        compiler_params=pltpu.CompilerParams(
            dimension_semantics=("arbitrary", "arbitrary"), vmem_limit_bytes=VMEM_LIMIT),
    )(*args)


def _moe_kernel(x_ref, h_ref, cmb_ref, grp_ref, mod_ref, before_ref, after_ref,
                w1_ref, w3_ref, w2_ref, o_ref, rank_col, rank_row, group_cmb, count, *, chunk, tail, lat_rows):
    tile = x_ref.shape[0]
    g = pl.program_id(2)
    n_exp = w1_ref.shape[0]

    @pl.when(g == 0)
    def _():
        o_ref[...] = x_ref[...]
        lane = lax.broadcasted_iota(jnp.int32, (1, LANES), 1)
        is_group = jnp.logical_and(lane >= N_EXPERTS, lane < N_EXPERTS + N_EXPERT_GROUPS)
        member_c = jnp.where(is_group, cmb_ref[...], 0.0)
        member_r = grp_ref[...]
        earlier_c = _dot(before_ref[...], member_c.astype(BF16))
        earlier_r = _dot(member_r.astype(BF16), after_ref[...])
        for gg in range(N_EXPERT_GROUPS):
            col = slice(N_EXPERTS + gg, N_EXPERTS + gg + 1)
            rank_col[gg] = jnp.broadcast_to(
                jnp.where(member_c[:, col] > 0.0, earlier_c[:, col], -1.0), (tile, LANES))
            rank_row[gg] = jnp.broadcast_to(
                jnp.where(member_r[gg:gg + 1] > 0.0, earlier_r[gg:gg + 1], -1.0), rank_row.shape[1:])
            group_cmb[gg] = cmb_ref[:, gg * EXPERTS_PER_GROUP:(gg + 1) * EXPERTS_PER_GROUP]
            count[gg] = jnp.sum(member_r[gg:gg + 1]).astype(jnp.int32)

    gate = mod_ref[1, N_MOD - 1:N_MOD]
    if lat_rows is not None:
        pos = pl.program_id(1) * tile + lax.broadcasted_iota(jnp.int32, (tile, 1), 0)
        gate = jnp.where(pos < lat_rows, gate, mod_ref[0, N_MOD - 1:N_MOD])
    cg = group_cmb[g]
    cg_hi = cg.astype(BF16)
    cg_lo = (cg - cg_hi.astype(F32)).astype(BF16)
    cg_parts = jnp.concatenate([cg_hi, cg_lo], axis=1)
    rrow = rank_row[g][0:1, :]
    rcol = rank_col[g]

    def run_chunk(first_rank, size):
        first = first_rank.astype(F32)
        want_r = lax.broadcasted_iota(jnp.int32, (size, 1), 0).astype(F32) + first
        gather = jnp.where(rrow == want_r, 1.0, 0.0).astype(BF16)
        h = _dot(gather, h_ref[...]).astype(BF16)
        cm = _dot(gather, cg_parts)
        cm = cm[:, :n_exp] + cm[:, n_exp:]
        acts = []
        for e in range(n_exp):
            a = _dot(h, w1_ref[e])
            a = (a * jax.nn.sigmoid(a)) * _dot(h, w3_ref[e])
            acts.append((a * cm[:, e:e + 1]).astype(BF16))
        y = _dot(jnp.concatenate(acts, axis=-1), w2_ref[...]).astype(BF16)
        want_c = lax.broadcasted_iota(jnp.int32, (1, size), 1).astype(F32) + first
        rank_lanes = jnp.concatenate([rcol] * (size // LANES), axis=1)
        scatter = jnp.where(rank_lanes == want_c, 1.0, 0.0).astype(BF16)
        o_ref[...] += gate * _dot(scatter, y)

    cnt = count[g]
    left = cnt % chunk
    n_full = cnt // chunk + jnp.where(left > tail, 1, 0)

    def full_chunk(c, carry):
        run_chunk(c * chunk, chunk)
        return carry

    lax.fori_loop(0, n_full, full_chunk, 0)

    @pl.when(jnp.logical_and(left > 0, left <= tail))
    def _():
        run_chunk(n_full * chunk, tail)


def _moe_tile(s):
    for tile, chunk in ((768, 256), (1024, 384), (512, 256), (256, 128)):
        if s % tile == 0:
            return tile, chunk
    raise ValueError(f"unsupported sequence length {s}")


def _moe(x_new, h2, cmb, grp, mod, lw, i, lat_rows):
    b, s, d = x_new.shape
    tile, chunk = _moe_tile(s)
    ge = EXPERTS_PER_GROUP
    tpos = jnp.arange(tile)
    before = (tpos[None, :] < tpos[:, None]).astype(BF16)
    tok = lambda bb, j, e: (bb, j, 0)
    kern = functools.partial(_moe_kernel, chunk=chunk, tail=LANES, lat_rows=lat_rows)
    return pl.pallas_call(
        kern,
        grid=(b, s // tile, N_EXPERT_GROUPS),
        in_specs=[
            pl.BlockSpec((None, tile, d), tok),
            pl.BlockSpec((None, tile, d), tok),
            pl.BlockSpec((None, tile, LANES), tok),
            pl.BlockSpec((None, 2 * N_EXPERT_GROUPS, tile), lambda bb, j, e: (bb, 0, j)),
            pl.BlockSpec((None, None, 2, N_MOD, d), lambda bb, j, e: (i, bb, 0, 0, 0)),
            _const_spec((tile, tile), lambda bb, j, e: (0, 0)),
            _const_spec((tile, tile), lambda bb, j, e: (0, 0)),
            pl.BlockSpec((None, ge, d, EXPERT_FF), lambda bb, j, e: (i, e, 0, 0)),
            pl.BlockSpec((None, ge, d, EXPERT_FF), lambda bb, j, e: (i, e, 0, 0)),
            pl.BlockSpec((None, ge * EXPERT_FF, d), lambda bb, j, e: (i, e, 0)),
        ],
        out_specs=pl.BlockSpec((None, tile, d), tok),
        out_shape=jax.ShapeDtypeStruct((b, s, d), F32),
        input_output_aliases={0: 0},
        scratch_shapes=[
            pltpu.VMEM((N_EXPERT_GROUPS, tile, LANES), F32),
            pltpu.VMEM((N_EXPERT_GROUPS, 8, tile), F32),
            pltpu.VMEM((N_EXPERT_GROUPS, tile, ge), F32),
            pltpu.SMEM((N_EXPERT_GROUPS,), jnp.int32),
        ],
        compiler_params=pltpu.CompilerParams(
            dimension_semantics=("arbitrary", "arbitrary", "arbitrary"), vmem_limit_bytes=VMEM_LIMIT),
    )(x_new, h2, cmb, grp, mod, before, before.T, lw["w1"], lw["w3"], lw["w2"])


def _prepare(p):
    dpt = p["w_in"].shape[0]
    w_in = p["w_in"].astype(BF16)
    zeros = lambda n: jnp.zeros((dpt, D_MODEL, n), BF16)
    kr_mid = KR_OFF + ROPE_HALF
    w_a = jnp.concatenate([
        w_in[..., KV_OFF:KR_OFF],
        zeros(NOPE_DIM), w_in[..., KR_OFF:Q_OFF], zeros(LANES - QK_DIM),
        zeros(NOPE_DIM), w_in[..., kr_mid:Q_OFF], w_in[..., KR_OFF:kr_mid], zeros(LANES - QK_DIM),
        w_in[..., Q_OFF:CONV_OFF]], axis=-1)
    gb_lo, gc_lo, u_lo = CONV_OFF, CONV_OFF + BRANCH_W, CONV_OFF + 2 * BRANCH_W
    w_h = jnp.concatenate([w_in[..., gc_lo:u_lo], w_in[..., u_lo:POOL_OFF],
                           w_in[..., POOL_OFF:SGU_OFF]], axis=-1)
    w_m = jnp.concatenate([w_in[..., gb_lo:gc_lo], w_in[..., SGU_OFF:]], axis=-1)
    w_ukv = p["w_ukv"].astype(BF16)
    w_uk = jnp.concatenate([w_ukv[..., :NOPE_DIM], jnp.zeros_like(w_ukv[..., :NOPE_DIM])], axis=-1)
    w_v = w_ukv[..., NOPE_DIM:]
    even = (jnp.arange(N_HEADS) % 2 == 0)[None, None, :, None]
    zero_v = jnp.zeros_like(w_v)
    w_uv = jnp.concatenate([jnp.where(even, w_v, zero_v), jnp.where(even, zero_v, w_v)], axis=-1)
    w_q = p["w_uq"].astype(BF16)
    w_uq = jnp.pad(w_q, ((0, 0), (0, 0), (0, 0), (0, LANES - QK_DIM)))
    rope_mid = NOPE_DIM + ROPE_HALF
    w_uqs = jnp.concatenate([
        jnp.zeros_like(w_q[..., :NOPE_DIM]), w_q[..., rope_mid:], w_q[..., NOPE_DIM:rope_mid],
        jnp.zeros_like(w_q[..., :LANES - QK_DIM])], axis=-1)
    eye = jnp.eye(SGU_TILE // SGU_CHUNK, dtype=F32)
    sgu_bd = jnp.einsum("ab,lgpq->lgapbq", eye, p["sgu_w"]).reshape(dpt, SGU_G, SGU_TILE, SGU_TILE)
    sgu_b = jnp.repeat(jnp.swapaxes(p["sgu_b"], 1, 2), SGU_CHUNK, axis=2)
    sgu_b = jnp.tile(sgu_b, (1, SGU_TILE // SGU_CHUNK, 1))
    return {
        "g1": p["norm1_g"][:, None, :], "g2": p["norm2_g"][:, None, :],
        "w_a": w_a, "w_h": w_h, "w_m": w_m,
        "g_kv": p["kv_a_norm_g"][:, None, :], "g_qa": p["q_a_norm_g"][:, None, :],
        "w_uk": w_uk.reshape(dpt, KV_LORA, N_HEADS * LANES),
        "w_uv": w_uv.reshape(dpt, KV_LORA, N_HEADS * LANES),
        "w_uq": w_uq.reshape(dpt, Q_LORA, N_HEADS * LANES),
        "w_uqs": w_uqs.reshape(dpt, Q_LORA, N_HEADS * LANES),
        "conv_w": p["conv_w"], "pool_w": p["pool_w"].astype(BF16),
        "pool_scale": p["pool_scale"][:, None, :], "g_sgu": p["sgu_norm_g"][:, None, :],
        "sgu_w": sgu_bd.astype(BF16), "sgu_b": sgu_b,
        "w_branch": p["w_branch"].astype(BF16), "w_o": p["w_o"].astype(BF16),
        "w1": p["moe_w1"].astype(BF16), "w3": p["moe_w3"].astype(BF16),
        "w2": p["moe_w2"].reshape(dpt, N_EXPERTS * EXPERT_FF, D_MODEL).astype(BF16),
    }


def _rope_tables(ctx_len, length, k_gain, q_gain):
    rows = length // GRID_W
    t = jnp.arange(rows * GRID_W)
    grid = jnp.stack([t // GRID_W, t % GRID_W], -1).astype(F32)
    inv_freq = ROPE_THETA ** (-jnp.arange(AXIS_PAIRS, dtype=F32) / AXIS_PAIRS)
    ang = jnp.concatenate([grid[:, :1] * inv_freq, grid[:, 1:] * inv_freq], -1)
    cos, sin = jnp.cos(ang), jnp.sin(ang)
    one = jnp.ones((length, NOPE_DIM), F32)
    tail = jnp.ones((length, LANES - QK_DIM), F32)
    zn = jnp.zeros((length, NOPE_DIM), F32)
    zt = jnp.zeros((length, LANES - QK_DIM), F32)
    rc = jnp.concatenate([one, cos, cos, tail], -1)
    rs = jnp.concatenate([zn, -sin, sin, zt], -1)
    ctx_id = lambda fill: jnp.full((ctx_len, LANES), fill, F32)
    rc = jnp.concatenate([rc, ctx_id(1.0)], 0)
    rs = jnp.concatenate([rs, ctx_id(0.0)], 0)

    def tables(gain, const):
        g = jnp.pad(gain, ((0, 0), (0, LANES - QK_DIM)))
        mid = NOPE_DIM + ROPE_HALF
        g_swap = jnp.concatenate([g[:, :NOPE_DIM], g[:, mid:QK_DIM], g[:, NOPE_DIM:mid],
                                  g[:, QK_DIM:]], -1)
        return (g[:, None, :] * rc[None]) * const, (g_swap[:, None, :] * rs[None]) * const

    gak, gbk = tables(k_gain, 1.0)
    gaq, gbq = tables(q_gain, ATTN_SCALE * LOG2E)
    return gak, gbk, gaq, gbq


def kernel(x, c, ctx, c_ctx, mod_w, mod_b, norm1_g, norm2_g, w_in, kv_a_norm_g, w_ukv, k_norm_g,
           q_a_norm_g, w_uq, q_norm_g, conv_w, pool_w, pool_scale, sgu_norm_g, sgu_w, sgu_b,
           w_branch, w_o, router_w, router_b, moe_w1, moe_w3, moe_w2):
    b, length, d = x.shape
    ctx_len = ctx.shape[1]
    assert d == D_MODEL and length % TOKEN_TILE == 0 and ctx_len % TOKEN_TILE == 0
    assert length % GRID_W == 0
    s = length + ctx_len

    lw = _prepare(dict(
        norm1_g=norm1_g, norm2_g=norm2_g, w_in=w_in, kv_a_norm_g=kv_a_norm_g, w_ukv=w_ukv,
        k_norm_g=k_norm_g, q_a_norm_g=q_a_norm_g, w_uq=w_uq, q_norm_g=q_norm_g, conv_w=conv_w,
        pool_w=pool_w, pool_scale=pool_scale, sgu_norm_g=sgu_norm_g, sgu_w=sgu_w, sgu_b=sgu_b,
        w_branch=w_branch, w_o=w_o, moe_w1=moe_w1, moe_w3=moe_w3, moe_w2=moe_w2))
    rope_tabs = _rope_tables(ctx_len, length, k_norm_g, q_norm_g)
    rw = router_w.T
    rb = router_b[:, None]

    rows = -(-(b + 1) // 8) * 8
    cvec = jnp.concatenate([c, c_ctx[None, :], jnp.zeros((rows - b - 1, d), F32)], 0)
    mod = _modulation(cvec, mod_w, mod_b).reshape(DEPTH, rows, N_MOD, d)
    mod = jnp.stack([jnp.broadcast_to(mod[:, b:b + 1], (DEPTH, b, N_MOD, d)), mod[:, :b]], axis=2)

    xs = jnp.concatenate([x, ctx], axis=1)
    for i in range(DEPTH):
        last = i == DEPTH - 1
        q, k, v = _qkv(xs, mod, lw, i, rope_tabs, length)
        out_len = length if last else s
        attn = _attention(q, k, v, 0, length, 0, out_len, None)
        if not last:
            attn = _attention(q, k, v, length, ctx_len, length, out_len, attn)
        outs = _mix(xs, mod, attn, lw, i, 0, length, 1, out_len, None, rw, rb)
        if not last:
            outs = _mix(xs, mod, attn, lw, i, length, ctx_len, 0, out_len, outs, rw, rb)
        x_new, h2, cmb, grp = outs
        xs = _moe(x_new, h2, cmb, grp, mod, lw, i, None if last else length)
    return xs
```
